```python
import math
import jax
import jax.numpy as jnp
from jax import lax
import numpy as np

D_MODEL = 1024
BATCH = 4
SEQ = 8192
DEPTH = 1

CHUNK = 64
Q_BLOCK = 128
PLE_DIM = 256
EPS = 1e-6
NEG_INF = -1e30

MLA_HEADS = 8
MLA_Q_RANK = 256
MLA_KV_RANK = 128
MLA_NOPE = 64
MLA_ROPE = 32
MLA_QK = MLA_NOPE + MLA_ROPE
MLA_V = 64
ROPE_THETA = 10000.0

DIFF_HEADS = 4
DIFF_QK = 64
DIFF_V = 2 * DIFF_QK

MIX_WIDTH = MLA_HEADS * MLA_V + DIFF_HEADS * DIFF_V

NUM_BUCKETS = 32
MAX_DISTANCE = 1024

D_FF = 2816
CONV_WIDTH = 3

OFF_Q_LAT = 0
OFF_KV_LAT = OFF_Q_LAT + MLA_Q_RANK
OFF_K_ROPE = OFF_KV_LAT + MLA_KV_RANK
OFF_DIFF_Q = OFF_K_ROPE + MLA_ROPE
OFF_DIFF_K = OFF_DIFF_Q + DIFF_HEADS * 2 * DIFF_QK
OFF_DIFF_V = OFF_DIFF_K + DIFF_HEADS * 2 * DIFF_QK
IN_COLS = OFF_DIFF_V + DIFF_HEADS * DIFF_V

kernel_name = 'hybrid_mla_diffattn_convffn_ple'


def lambda_init(layer):
    return 0.8 - 0.6 * math.exp(-0.3 * layer)


def rms_norm(x, g):
    xf = x.astype(jnp.float32)
    y = xf * lax.rsqrt(jnp.mean(xf * xf, axis=-1, keepdims=True) + EPS)
    return (y * g.astype(jnp.float32)).astype(x.dtype)


def apply_rope(t, pos):
    half = t.shape[-1] // 2
    inv_freq = ROPE_THETA ** (-jnp.arange(half, dtype=jnp.float32) / half)
    ang = pos.astype(jnp.float32)[:, None] * inv_freq[None, :]
    cos = jnp.cos(ang)[None, :, None, :].astype(t.dtype)
    sin = jnp.sin(ang)[None, :, None, :].astype(t.dtype)
    t1, t2 = t[..., :half], t[..., half:]
    return jnp.concatenate([t1 * cos - t2 * sin, t2 * cos + t1 * sin], axis=-1)


def t5_bucket(rel):
    nb = NUM_BUCKETS // 2
    max_exact = nb // 2
    sign_off = (rel > 0).astype(jnp.int32) * nb
    n = jnp.abs(rel)
    nf = jnp.maximum(n, 1).astype(jnp.float32)
    large = max_exact + (jnp.log(nf / max_exact) / math.log(MAX_DISTANCE / max_exact)
                         * (nb - max_exact)).astype(jnp.int32)
    large = jnp.minimum(large, nb - 1)
    return sign_off + jnp.where(n < max_exact, n, large)


def block_indices(blk, seq):
    q_idx = blk * Q_BLOCK + jnp.arange(Q_BLOCK, dtype=jnp.int32)
    k_idx = jnp.arange(seq, dtype=jnp.int32)
    return q_idx, k_idx


def chunk_allowed(q_idx, k_idx):
    return (k_idx[None, :] // CHUNK) <= (q_idx[:, None] // CHUNK)


def to_blocks(t):
    b, s, h, d = t.shape
    return t.reshape(b, s // Q_BLOCK, Q_BLOCK, h, d).transpose(1, 0, 3, 2, 4)


def from_blocks(t):
    n, b, h, qb, d = t.shape
    return t.transpose(1, 0, 3, 2, 4).reshape(b, n * qb, h, d)


def mla_group(q_lat, kv_lat, k_rope, pos, q_lat_g, w_uq, kv_lat_g, w_ukv, q_g, k_g):
    b, s, _ = q_lat.shape
    q = (rms_norm(q_lat, q_lat_g) @ w_uq).reshape(b, s, MLA_HEADS, MLA_QK)
    kv = (rms_norm(kv_lat, kv_lat_g) @ w_ukv).reshape(b, s, MLA_HEADS, MLA_NOPE + MLA_V)
    k_nope, v = kv[..., :MLA_NOPE], kv[..., MLA_NOPE:]
    k_r = jnp.broadcast_to(k_rope[:, :, None, :], (b, s, MLA_HEADS, MLA_ROPE))
    k = jnp.concatenate([k_nope, k_r], axis=-1)
    q = rms_norm(q, q_g)
    k = rms_norm(k, k_g)
    q = jnp.concatenate([q[..., :MLA_NOPE], apply_rope(q[..., MLA_NOPE:], pos)], axis=-1)
    k = jnp.concatenate([k[..., :MLA_NOPE], apply_rope(k[..., MLA_NOPE:], pos)], axis=-1)
    k_t = k.transpose(0, 2, 1, 3)
    v_t = v.transpose(0, 2, 1, 3)
    scale = MLA_QK ** -0.5

    def one_block(args):
        qb, blk = args
        q_idx, k_idx = block_indices(blk, s)
        allowed = chunk_allowed(q_idx, k_idx)
        logits = jnp.einsum('bhqd,bhkd->bhqk', qb, k_t).astype(jnp.float32) * scale
        logits = jnp.where(allowed[None, None], logits, NEG_INF)
        probs = jax.nn.softmax(logits, axis=-1).astype(v_t.dtype)
        return jnp.einsum('bhqk,bhkd->bhqd', probs, v_t)

    o = lax.map(one_block, (to_blocks(q), jnp.arange(s // Q_BLOCK, dtype=jnp.int32)))
    return from_blocks(o).reshape(b, s, MLA_HEADS * MLA_V)


def diff_group(dq, dk, dv, q_g, k_g, lq1, lk1, lq2, lk2, out_g, rel_bias, lam_init):
    b, s, _ = dq.shape
    q = rms_norm(dq.reshape(b, s, DIFF_HEADS, 2, DIFF_QK), q_g)
    k = rms_norm(dk.reshape(b, s, DIFF_HEADS, 2, DIFF_QK), k_g)
    k_t = k.transpose(0, 2, 3, 1, 4)
    v_t = dv.reshape(b, s, DIFF_HEADS, DIFF_V).transpose(0, 2, 1, 3)
    f32 = jnp.float32
    lam = (jnp.exp(jnp.sum(lq1.astype(f32) * lk1.astype(f32)))
           - jnp.exp(jnp.sum(lq2.astype(f32) * lk2.astype(f32))) + lam_init)
    scale = DIFF_QK ** -0.5
    table = rel_bias.astype(f32)

    def one_block(args):
        qb, blk = args
        qb = qb.reshape(b, DIFF_HEADS, Q_BLOCK, 2, DIFF_QK)
        q_idx, k_idx = block_indices(blk, s)
        allowed = chunk_allowed(q_idx, k_idx)
        bias = table[t5_bucket(k_idx[None, :] - q_idx[:, None])].transpose(2, 0, 1)
        logits = jnp.einsum('bhqmd,bhmkd->bhmqk', qb, k_t).astype(f32) * scale
        logits = logits + bias[None, :, None]
        logits = jnp.where(allowed[None, None, None], logits, NEG_INF)
        probs = jax.nn.softmax(logits, axis=-1)
        attn = (probs[:, :, 0] - lam * probs[:, :, 1]).astype(v_t.dtype)
        return jnp.einsum('bhqk,bhkd->bhqd', attn, v_t)

    qf = q.reshape(b, s, DIFF_HEADS, 2 * DIFF_QK)
    o = lax.map(one_block, (to_blocks(qf), jnp.arange(s // Q_BLOCK, dtype=jnp.int32)))
    o = rms_norm(from_blocks(o), out_g) * (1.0 - lam_init)
    return o.reshape(b, s, DIFF_HEADS * DIFF_V)


def conv_ffn(h, w_gate, w_up, conv_w, conv_b, w_down):
    s = h.shape[1]
    g = h @ w_gate
    gp = jnp.pad(g, ((0, 0), (CONV_WIDTH - 1, 0), (0, 0)))
    conv = conv_b
    for j in range(CONV_WIDTH):
        conv = conv + gp[:, j:j + s, :] * conv_w[j]
    return (jax.nn.silu(conv) * (h @ w_up)) @ w_down


def setup_inputs(seed: int = 0) -> dict:
    key = jax.random.key(seed)
    ks = jax.random.split(key, 28)
    f32 = jnp.float32

    def nrm(k, shape, scale):
        return jax.random.normal(k, shape, dtype=f32) * scale

    def gain(k, shape):
        return 1.0 + 0.1 * jax.random.normal(k, shape, dtype=f32)

    return {
        'x': nrm(ks[0], (BATCH, SEQ, D_MODEL), 1.0),
        'p': nrm(ks[1], (DEPTH, BATCH, SEQ, PLE_DIM), 1.0),
        'attn_norm_g': gain(ks[2], (DEPTH, D_MODEL)),
        'w_in': nrm(ks[3], (DEPTH, D_MODEL, IN_COLS), D_MODEL ** -0.5),
        'q_lat_norm_g': gain(ks[4], (DEPTH, MLA_Q_RANK)),
        'w_uq': nrm(ks[5], (DEPTH, MLA_Q_RANK, MLA_HEADS * MLA_QK), MLA_Q_RANK ** -0.5),
        'kv_lat_norm_g': gain(ks[6], (DEPTH, MLA_KV_RANK)),
        'w_ukv': nrm(ks[7], (DEPTH, MLA_KV_RANK, MLA_HEADS * (MLA_NOPE + MLA_V)), MLA_KV_RANK ** -0.5),
        'mla_q_norm_g': gain(ks[8], (DEPTH, MLA_QK)),
        'mla_k_norm_g': gain(ks[9], (DEPTH, MLA_QK)),
        'diff_q_norm_g': gain(ks[10], (DEPTH, DIFF_QK)),
        'diff_k_norm_g': gain(ks[11], (DEPTH, DIFF_QK)),
        'lambda_q1': nrm(ks[12], (DEPTH, DIFF_QK), 0.1),
        'lambda_k1': nrm(ks[13], (DEPTH, DIFF_QK), 0.1),
        'lambda_q2': nrm(ks[14], (DEPTH, DIFF_QK), 0.1),
        'lambda_k2': nrm(ks[15], (DEPTH, DIFF_QK), 0.1),
        'diff_out_norm_g': gain(ks[16], (DEPTH, DIFF_V)),
        'rel_bias': nrm(ks[17], (NUM_BUCKETS, DIFF_HEADS), 0.5),
        'w_out': nrm(ks[18], (DEPTH, MIX_WIDTH, D_MODEL), MIX_WIDTH ** -0.5),
        'ffn_norm_g': gain(ks[19], (DEPTH, D_MODEL)),
        'w_gate': nrm(ks[20], (DEPTH, D_MODEL, D_FF), D_MODEL ** -0.5),
        'w_up': nrm(ks[21], (DEPTH, D_MODEL, D_FF), D_MODEL ** -0.5),
        'conv_w': nrm(ks[22], (DEPTH, CONV_WIDTH, D_FF), CONV_WIDTH ** -0.5),
        'conv_b': nrm(ks[23], (DEPTH, D_FF), 0.02),
        'w_down': nrm(ks[24], (DEPTH, D_FF, D_MODEL), D_FF ** -0.5),
        'ple_norm_g': gain(ks[25], (DEPTH, D_MODEL)),
        'w_ple_gate': nrm(ks[26], (DEPTH, D_MODEL, D_MODEL), D_MODEL ** -0.5),
        'w_ple_proj': nrm(ks[27], (DEPTH, PLE_DIM, D_MODEL), PLE_DIM ** -0.5),
    }


def reference(x, p, attn_norm_g, w_in, q_lat_norm_g, w_uq, kv_lat_norm_g, w_ukv,
              mla_q_norm_g, mla_k_norm_g, diff_q_norm_g, diff_k_norm_g,
              lambda_q1, lambda_k1, lambda_q2, lambda_k2, diff_out_norm_g, rel_bias,
              w_out, ffn_norm_g, w_gate, w_up, conv_w, conv_b, w_down,
              ple_norm_g, w_ple_gate, w_ple_proj):
    s = x.shape[1]
    pos = jnp.arange(s, dtype=jnp.int32)
    for i in range(DEPTH):
        h = rms_norm(x, attn_norm_g[i])
        z = h @ w_in[i]
        y_mla = mla_group(z[..., OFF_Q_LAT:OFF_KV_LAT], z[..., OFF_KV_LAT:OFF_K_ROPE],
                          z[..., OFF_K_ROPE:OFF_DIFF_Q], pos,
                          q_lat_norm_g[i], w_uq[i], kv_lat_norm_g[i], w_ukv[i],
                          mla_q_norm_g[i], mla_k_norm_g[i])
        y_diff = diff_group(z[..., OFF_DIFF_Q:OFF_DIFF_K], z[..., OFF_DIFF_K:OFF_DIFF_V],
                            z[..., OFF_DIFF_V:IN_COLS],
                            diff_q_norm_g[i], diff_k_norm_g[i],
                            lambda_q1[i], lambda_k1[i], lambda_q2[i], lambda_k2[i],
                            diff_out_norm_g[i], rel_bias, lambda_init(i))
        x = x + jnp.concatenate([y_mla, y_diff], axis=-1) @ w_out[i]
        x = x + conv_ffn(rms_norm(x, ffn_norm_g[i]), w_gate[i], w_up[i], conv_w[i], conv_b[i], w_down[i])
        gate = jax.nn.sigmoid(rms_norm(x, ple_norm_g[i]) @ w_ple_gate[i])
        x = x + gate * (p[i] @ w_ple_proj[i])
    return x
```

```python
import functools
import math

import jax
import jax.numpy as jnp
from jax import lax
from jax.experimental import pallas as pl
from jax.experimental.pallas import tpu as pltpu

F32 = jnp.float32
BF16 = jnp.bfloat16

LANES = 128
SUBLANES = 8
VMEM_LIMIT_BYTES = 56 * 1024 * 1024

CHUNK = 64
EPS = 1e-6
NEG_INF = -1e30
MLA_HEADS = 8
MLA_Q_RANK = 256
MLA_KV_RANK = 128
MLA_NOPE = 64
MLA_ROPE = 32
MLA_QK = MLA_NOPE + MLA_ROPE
MLA_V = 64
ROPE_THETA = 10000.0
DIFF_HEADS = 4
DIFF_QK = 64
DIFF_V = 2 * DIFF_QK
NUM_BUCKETS = 32
MAX_DISTANCE = 1024
CONV_WIDTH = 3
LAMBDA_INIT = 0.8 - 0.6 * math.exp(-0.3 * 0)
LOG2E = math.log2(math.e)

TQ = 512
TK = 256
KSPLIT = 2
KSTEP = TK * KSPLIT
NEAR_KEYS = 1024
MLA_HPS = 4
DIFF_HPS = 2
IN_ROW_TILE = 512
ROW_TILE = 512
FFN_ROW_TILE = 256
HALO_ROWS = SUBLANES

ZC_QLAT = 0
ZC_KVLAT = ZC_QLAT + MLA_Q_RANK
ZC_KR = ZC_KVLAT + MLA_KV_RANK
ZC_KRSW = ZC_KR + LANES
ZC_DQ = ZC_KRSW + LANES
ZC_DK = ZC_DQ + DIFF_HEADS * DIFF_V
ZC_END = ZC_DK + DIFF_HEADS * DIFF_V

NT_DIMS = (((1,), (1,)), ((), ()))


def _params(*sem):
    return pltpu.CompilerParams(dimension_semantics=sem, vmem_limit_bytes=VMEM_LIMIT_BYTES)


def _rms(x, width):
    return lax.rsqrt(jnp.sum(x * x, axis=-1, keepdims=True) * (1.0 / width) + EPS)


def _bias_kernel(tab_ref, bias_ref, mask_ref):
    t = TQ
    kk = lax.broadcasted_iota(jnp.int32, (t, t), 0)
    qq = lax.broadcasted_iota(jnp.int32, (t, t), 1)
    chunk_bits = CHUNK.bit_length() - 1
    q_chunk = lax.shift_right_logical(qq, chunk_bits)
    mask_ref[...] = jnp.where(lax.shift_right_logical(kk, chunk_bits) <= q_chunk, 0.0, NEG_INF).astype(F32)
    key_off = kk + pl.program_id(0) * t - NEAR_KEYS
    add_mask = jnp.where(lax.shift_right_arithmetic(key_off, chunk_bits) <= q_chunk, 0.0, NEG_INF).astype(F32)
    rel = key_off - qq
    nb = NUM_BUCKETS // 2
    max_exact = nb // 2
    sign_off = (rel > 0).astype(jnp.int32) * nb
    n = jnp.abs(rel)
    nf = jnp.maximum(n, 1).astype(F32)
    large = max_exact + (jnp.log(nf / max_exact) / math.log(MAX_DISTANCE / max_exact)
                         * (nb - max_exact)).astype(jnp.int32)
    large = jnp.minimum(large, nb - 1)
    bucket = sign_off + jnp.where(n < max_exact, n, large)
    for h in range(DIFF_HEADS):
        acc = jnp.zeros((t, t), F32)
        for b in range(NUM_BUCKETS):
            acc = jnp.where(bucket == b, tab_ref[b, h], acc)
        bias_ref[h] = acc * LOG2E + add_mask


def _bias_tiles(rel_bias):
    t = TQ
    n_blocks = (NEAR_KEYS + TQ) // t
    return pl.pallas_call(
        _bias_kernel,
        grid=(n_blocks,),
        in_specs=[pl.BlockSpec(memory_space=pltpu.SMEM)],
        out_specs=[pl.BlockSpec((DIFF_HEADS, t, t), lambda d: (0, d, 0)),
                   pl.BlockSpec((t, t), lambda d: (0, 0))],
        out_shape=[jax.ShapeDtypeStruct((DIFF_HEADS, n_blocks * t, t), F32),
                   jax.ShapeDtypeStruct((t, t), F32)],
        compiler_params=_params("arbitrary"),
        name="bias_tiles",
    )(rel_bias)


def _in_proj_kernel(x_ref, cos_ref, sin_ref, g_attn_ref, w_in_ref, w_dvt_ref, g_ql_ref, w_uq_ref, w_uqsw_ref,
                    g_kvl_ref, w_uk_ref, w_uvt_ref, gq_ref, gqsw_ref, gk_ref, gksw_ref,
                    gdq_ref, gdk_ref,
                    qm_ref, km_ref, vmt_ref, qd_ref, kd_ref, vdt_ref):
    x = x_ref[...]
    h = (x * _rms(x, x.shape[-1]) * g_attn_ref[...]).astype(BF16)
    z = jnp.dot(h, w_in_ref[...], preferred_element_type=F32)

    def store_transposed(dst_ref, w_t, act):
        v_t = lax.dot_general(w_t, act, NT_DIMS, preferred_element_type=F32).astype(BF16)
        for c in range(v_t.shape[1] // TK):
            dst_ref[0, c] = v_t[:, c * TK:(c + 1) * TK]

    store_transposed(vdt_ref, w_dvt_ref[...], h)

    q_lat = z[:, ZC_QLAT:ZC_KVLAT]
    kv_lat = z[:, ZC_KVLAT:ZC_KR]
    kr = z[:, ZC_KR:ZC_KRSW]
    krsw = z[:, ZC_KRSW:ZC_DQ]

    qln = (q_lat * _rms(q_lat, MLA_Q_RANK) * g_ql_ref[...]).astype(BF16)
    q = jnp.dot(qln, w_uq_ref[...], preferred_element_type=F32)
    qsw = jnp.dot(qln, w_uqsw_ref[...], preferred_element_type=F32)
    kvn = (kv_lat * _rms(kv_lat, MLA_KV_RANK) * g_kvl_ref[...]).astype(BF16)
    kn = jnp.dot(kvn, w_uk_ref[...], preferred_element_type=F32)
    store_transposed(vmt_ref, w_uvt_ref[...], kvn)

    cos = cos_ref[...]
    sin = sin_ref[...]
    q_scale = MLA_QK ** -0.5 * LOG2E
    q_cos = cos * (gq_ref[...] * q_scale)
    q_sin = sin * (gqsw_ref[...] * q_scale)
    k_cos = cos * gk_ref[...]
    k_sin = sin * gksw_ref[...]
    for hd in range(MLA_HEADS):
        sl = slice(hd * LANES, (hd + 1) * LANES)
        qh = q[:, sl]
        qm_ref[:, sl] = (_rms(qh, MLA_QK) * (qh * q_cos + qsw[:, sl] * q_sin)).astype(BF16)
        kh = kn[:, sl] + kr
        km_ref[:, sl] = (_rms(kh, MLA_QK) * (kh * k_cos + krsw * k_sin)).astype(BF16)

    lane = lax.broadcasted_iota(jnp.int32, (1, LANES), 1)
    first_map = lane < DIFF_QK
    d_scale = DIFF_QK ** -0.5 * LOG2E
    for hd in range(DIFF_HEADS):
        sl = slice(hd * LANES, (hd + 1) * LANES)
        for src, g_ref, dst, scale in ((ZC_DQ, gdq_ref, qd_ref, d_scale), (ZC_DK, gdk_ref, kd_ref, 1.0)):
            blk = z[:, src + hd * LANES: src + (hd + 1) * LANES]
            sq = blk * blk
            tot = jnp.sum(sq, axis=-1, keepdims=True)
            lo = jnp.sum(jnp.where(first_map, sq, 0.0), axis=-1, keepdims=True)
            ms = jnp.where(first_map, lo, tot - lo) * (1.0 / DIFF_QK)
            dst[:, sl] = (blk * lax.rsqrt(ms + EPS) * (g_ref[...] * scale)).astype(BF16)


def _in_proj(x2, cos_t, sin_t, g_attn, w_in_p, w_dvt, g_ql, w_uq_p, w_uq_sw, g_kvl, w_uk_p, w_uvt,
             gq, gqsw, gk, gksw, gdq, gdk, batch, seq):
    n, dm = x2.shape
    tm = IN_ROW_TILE
    tiles_per_seq = seq // tm
    row = lambda i: (i, 0)
    const = lambda i: (0, 0)
    pos = lambda i: (i % tiles_per_seq, 0)
    tile4 = lambda i: (i // tiles_per_seq, i % tiles_per_seq, 0, 0)
    kt = tm // TK

    def full(a):
        return pl.BlockSpec(a.shape, const)

    def rows_out(width):
        return pl.BlockSpec((tm, width), row), jax.ShapeDtypeStruct((n, width), BF16)

    def transposed_out(width):
        return (pl.BlockSpec((1, kt, width, TK), tile4),
                jax.ShapeDtypeStruct((batch, tiles_per_seq * kt, width, TK), BF16))

    outs = [rows_out(MLA_HEADS * LANES), rows_out(MLA_HEADS * LANES), transposed_out(MLA_HEADS * MLA_V),
            rows_out(DIFF_HEADS * DIFF_V), rows_out(DIFF_HEADS * DIFF_V), transposed_out(DIFF_HEADS * DIFF_V)]
    return pl.pallas_call(
        _in_proj_kernel,
        grid=(n // tm,),
        in_specs=[pl.BlockSpec((tm, dm), row), pl.BlockSpec((tm, LANES), pos), pl.BlockSpec((tm, LANES), pos),
                  full(g_attn), full(w_in_p), full(w_dvt), full(g_ql), full(w_uq_p), full(w_uq_sw),
                  full(g_kvl), full(w_uk_p), full(w_uvt), full(gq), full(gqsw), full(gk), full(gksw),
                  full(gdq), full(gdk)],
        out_specs=[o[0] for o in outs],
        out_shape=[o[1] for o in outs],
        compiler_params=_params("parallel"),
        name="in_proj",
    )(x2, cos_t, sin_t, g_attn, w_in_p, w_dvt, g_ql, w_uq_p, w_uq_sw, g_kvl, w_uk_p, w_uvt,
      gq, gqsw, gk, gksw, gdq, gdk)


def _softmax_step(s_tiles, shift, m, l, acc, vt_tiles):
    col_max = functools.reduce(jnp.maximum, [jnp.max(s, axis=0, keepdims=True) for s in s_tiles])
    if shift is not None:
        col_max = col_max + shift
    m_new = jnp.maximum(m, col_max)
    alpha = jnp.exp2(m - m_new)
    m_sub = m_new if shift is None else m_new - shift
    l_new = alpha * l
    acc_new = alpha * acc
    for s, vt in zip(s_tiles, vt_tiles):
        p = jnp.exp2(s - m_sub)
        l_new = l_new + jnp.sum(p, axis=0, keepdims=True)
        acc_new = acc_new + jnp.dot(vt, p.astype(BF16), preferred_element_type=F32)
    return m_new, l_new, acc_new


def _key_rows(step_idx, c):
    return pl.ds(pl.multiple_of(step_idx * KSTEP + c * TK, TK), TK)


def _init_state(tq, dv):
    return (jnp.full((1, tq), NEG_INF, F32), jnp.zeros((1, tq), F32), jnp.zeros((dv, tq), F32))


def _mla_kernel(q_ref, k_ref, vt_ref, mask_ref, o_ref):
    qi = pl.program_id(2)
    qs = [q_ref[0, :, hh * LANES:(hh + 1) * LANES] for hh in range(MLA_HPS)]

    def step(j, carry, masked):
        scores = []
        for hh in range(MLA_HPS):
            s_tiles = []
            for c in range(KSPLIT):
                s = lax.dot_general(k_ref[0, _key_rows(j, c), hh * LANES:(hh + 1) * LANES], qs[hh], NT_DIMS,
                                    preferred_element_type=F32)
                if masked:
                    off = pl.multiple_of((j * KSPLIT + c) * TK - qi * TQ, TK)
                    s = s + mask_ref[pl.ds(off, TK), :]
                s_tiles.append(s)
            scores.append(s_tiles)
        out = []
        for hh in range(MLA_HPS):
            vt_tiles = [vt_ref[0, j * KSPLIT + c, hh * MLA_V:(hh + 1) * MLA_V, :] for c in range(KSPLIT)]
            out.append(_softmax_step(scores[hh], None, *carry[hh], vt_tiles))
        return tuple(out)

    steps_per_q = TQ // KSTEP
    init = tuple(_init_state(TQ, MLA_V) for _ in range(MLA_HPS))
    carry = lax.fori_loop(0, qi * steps_per_q, lambda j, c: step(j, c, False), init)
    carry = lax.fori_loop(qi * steps_per_q, (qi + 1) * steps_per_q, lambda j, c: step(j, c, True), carry)
    o_t = jnp.concatenate([a / l for (_, l, a) in carry], axis=0)
    o_ref[0] = o_t.T.astype(o_ref.dtype)


def _mla_attention(qm, km, vmt, mask):
    b, s, _ = qm.shape
    t = TQ
    pairs = MLA_HEADS // MLA_HPS
    return pl.pallas_call(
        _mla_kernel,
        grid=(b, pairs, s // t),
        in_specs=[pl.BlockSpec((1, t, MLA_HPS * LANES), lambda bi, hp, qi: (bi, qi, hp)),
                  pl.BlockSpec((1, s, MLA_HPS * LANES), lambda bi, hp, qi: (bi, 0, hp)),
                  pl.BlockSpec((1, s // TK, MLA_HPS * MLA_V, TK), lambda bi, hp, qi: (bi, 0, hp, 0)),
                  pl.BlockSpec((t, t), lambda bi, hp, qi: (0, 0))],
        out_specs=pl.BlockSpec((1, t, MLA_HPS * MLA_V), lambda bi, hp, qi: (bi, qi, hp)),
        out_shape=jax.ShapeDtypeStruct((b, s, MLA_HEADS * MLA_V), BF16),
        compiler_params=_params("parallel", "parallel", "arbitrary"),
        name="mla_attn",
    )(qm, km, vmt, mask)


def _diff_kernel(tab_ref, q_ref, k_ref, vt_ref, bias_ref, lam_ref, g_out_ref, o_ref):
    hp = pl.program_id(1)
    qi = pl.program_id(2)
    lane = lax.broadcasted_iota(jnp.int32, (1, LANES), 1)
    qs = []
    for hh in range(DIFF_HPS):
        q = q_ref[0, :, hh * LANES:(hh + 1) * LANES]
        zero = jnp.zeros_like(q)
        qs.append([jnp.where(lane < DIFF_QK, q, zero), jnp.where(lane >= DIFF_QK, q, zero)])
    far_bias = [tab_ref[NUM_BUCKETS // 2 - 1, hp * DIFF_HPS + hh] * LOG2E for hh in range(DIFF_HPS)]
    strip_start = qi * TQ - NEAR_KEYS

    def step(j, carry, near):
        scores = []
        for hh in range(DIFF_HPS):
            ks = [k_ref[0, _key_rows(j, c), hh * LANES:(hh + 1) * LANES] for c in range(KSPLIT)]
            for mp in range(2):
                s_tiles = []
                for c in range(KSPLIT):
                    s = lax.dot_general(ks[c], qs[hh][mp], NT_DIMS, preferred_element_type=F32)
                    if near:
                        off = pl.multiple_of((j * KSPLIT + c) * TK - strip_start, TK)
                        s = s + bias_ref[hh, pl.ds(off, TK), :]
                    s_tiles.append(s)
                scores.append(s_tiles)
        out = []
        for hh in range(DIFF_HPS):
            vt_tiles = [vt_ref[0, j * KSPLIT + c, hh * DIFF_V:(hh + 1) * DIFF_V, :] for c in range(KSPLIT)]
            for mp in range(2):
                idx = hh * 2 + mp
                out.append(_softmax_step(scores[idx], None if near else far_bias[hh], *carry[idx], vt_tiles))
        return tuple(out)

    n_far = jnp.maximum(qi * (TQ // KSTEP) - NEAR_KEYS // KSTEP, 0)
    n_all = (qi + 1) * (TQ // KSTEP)
    init = tuple(_init_state(TQ, DIFF_V) for _ in range(2 * DIFF_HPS))
    carry = lax.fori_loop(0, n_far, lambda j, c: step(j, c, False), init)
    carry = lax.fori_loop(n_far, n_all, lambda j, c: step(j, c, True), carry)

    lv = lam_ref[...]
    lam = (jnp.exp(jnp.sum(lv[0:1] * lv[1:2], axis=-1, keepdims=True))
           - jnp.exp(jnp.sum(lv[2:3] * lv[3:4], axis=-1, keepdims=True)) + LAMBDA_INIT)
    for hh in range(DIFF_HPS):
        (_, l0, a0), (_, l1, a1) = carry[2 * hh], carry[2 * hh + 1]
        o_t = a0 / l0 - lam * (a1 / l1)
        ms = jnp.sum(o_t * o_t, axis=0, keepdims=True) * (1.0 / DIFF_V)
        o_t = o_t * lax.rsqrt(ms + EPS)
        o_ref[0, :, hh * LANES:(hh + 1) * LANES] = (
            o_t.T * g_out_ref[...] * (1.0 - LAMBDA_INIT)).astype(o_ref.dtype)


def _diff_attention(rel_bias, qd, kd, vdt, bias, lam_vecs, g_out):
    b, s, _ = qd.shape
    t = TQ
    return pl.pallas_call(
        _diff_kernel,
        grid=(b, DIFF_HEADS // DIFF_HPS, s // t),
        in_specs=[pl.BlockSpec(memory_space=pltpu.SMEM),
                  pl.BlockSpec((1, t, DIFF_HPS * LANES), lambda bi, hd, qi: (bi, qi, hd)),
                  pl.BlockSpec((1, s, DIFF_HPS * LANES), lambda bi, hd, qi: (bi, 0, hd)),
                  pl.BlockSpec((1, s // TK, DIFF_HPS * DIFF_V, TK), lambda bi, hd, qi: (bi, 0, hd, 0)),
                  pl.BlockSpec((DIFF_HPS, NEAR_KEYS + t, t), lambda bi, hd, qi: (hd, 0, 0)),
                  pl.BlockSpec(lam_vecs.shape, lambda bi, hd, qi: (0, 0)),
                  pl.BlockSpec(g_out.shape, lambda bi, hd, qi: (0, 0))],
        out_specs=pl.BlockSpec((1, t, DIFF_HPS * LANES), lambda bi, hd, qi: (bi, qi, hd)),
        out_shape=jax.ShapeDtypeStruct((b, s, DIFF_HEADS * DIFF_V), BF16),
        compiler_params=_params("parallel", "parallel", "arbitrary"),
        name="diff_attn",
    )(rel_bias, qd, kd, vdt, bias, lam_vecs, g_out)


def _out_proj_kernel(x_ref, ym_ref, yd_ref, wm_ref, wd_ref, g_ref, x1_ref, h2_ref):
    x1 = (x_ref[...]
          + jnp.dot(ym_ref[...], wm_ref[...], preferred_element_type=F32)
          + jnp.dot(yd_ref[...], wd_ref[...], preferred_element_type=F32))
    x1_ref[...] = x1
    h2_ref[...] = (x1 * _rms(x1, x1.shape[-1]) * g_ref[...]).astype(BF16)


def _out_proj(x2, ym, yd, w_out_m, w_out_d, g_ffn):
    n, dm = x2.shape
    tm = ROW_TILE
    row = lambda i: (i, 0)
    const = lambda i: (0, 0)
    return pl.pallas_call(
        _out_proj_kernel,
        grid=(n // tm,),
        in_specs=[pl.BlockSpec((tm, dm), row), pl.BlockSpec((tm, ym.shape[1]), row),
                  pl.BlockSpec((tm, yd.shape[1]), row), pl.BlockSpec(w_out_m.shape, const),
                  pl.BlockSpec(w_out_d.shape, const), pl.BlockSpec(g_ffn.shape, const)],
        out_specs=[pl.BlockSpec((tm, dm), row), pl.BlockSpec((tm, dm), row)],
        out_shape=[jax.ShapeDtypeStruct((n, dm), F32), jax.ShapeDtypeStruct((n, dm), BF16)],
        compiler_params=_params("parallel"),
        name="out_proj",
    )(x2, ym, yd, w_out_m, w_out_d, g_ffn)


def _ffn_kernel(tiles_per_seq, x1_ref, h2_ref, halo_ref, p_ref, wg_ref, wu_ref, cw_ref, cb_ref, wd_ref,
                g_ple_ref, wpg_ref, wpp_ref, o_ref, g_scr):
    tm = x1_ref.shape[0]
    h2 = h2_ref[...]
    wg = wg_ref[...]
    keep = (pl.program_id(0) % tiles_per_seq != 0).astype(F32)
    g_scr[0:HALO_ROWS, :] = jnp.dot(halo_ref[...], wg, preferred_element_type=F32) * keep
    g_scr[HALO_ROWS:, :] = jnp.dot(h2, wg, preferred_element_type=F32)
    conv = cb_ref[...]
    for j in range(CONV_WIDTH):
        start = HALO_ROWS - (CONV_WIDTH - 1) + j
        conv = conv + g_scr[start:start + tm, :] * cw_ref[j:j + 1, :]
    up = jnp.dot(h2, wu_ref[...], preferred_element_type=F32)
    act = (conv * jax.nn.sigmoid(conv) * up).astype(BF16)
    x2 = x1_ref[...] + jnp.dot(act, wd_ref[...], preferred_element_type=F32)
    hn = (x2 * _rms(x2, x2.shape[-1]) * g_ple_ref[...]).astype(BF16)
    gate = jax.nn.sigmoid(jnp.dot(hn, wpg_ref[...], preferred_element_type=F32))
    proj = jnp.dot(p_ref[...].astype(BF16), wpp_ref[...], preferred_element_type=F32)
    o_ref[...] = x2 + gate * proj


def _ffn_ple(x1, h2, p2, w_gate, w_up, conv_w, conv_b, w_down, g_ple, w_pg, w_pp, seq):
    n, dm = x1.shape
    tm = min(FFN_ROW_TILE, seq)
    d_ff = w_gate.shape[1]
    row = lambda i: (i, 0)
    const = lambda i: (0, 0)
    halo = lambda i: (jnp.maximum(i * (tm // HALO_ROWS) - 1, 0), 0)

    def full(a):
        return pl.BlockSpec(a.shape, const)

    return pl.pallas_call(
        functools.partial(_ffn_kernel, seq // tm),
        grid=(n // tm,),
        in_specs=[pl.BlockSpec((tm, dm), row), pl.BlockSpec((tm, dm), row),
                  pl.BlockSpec((HALO_ROWS, dm), halo), pl.BlockSpec((tm, p2.shape[1]), row),
                  full(w_gate), full(w_up), full(conv_w), full(conv_b), full(w_down),
                  full(g_ple), full(w_pg), full(w_pp)],
        out_specs=pl.BlockSpec((tm, dm), row),
        out_shape=jax.ShapeDtypeStruct((n, dm), F32),
        scratch_shapes=[pltpu.VMEM((tm + HALO_ROWS, d_ff), F32)],
        compiler_params=_params("parallel"),
        name="ffn_ple",
    )(x1, h2, h2, p2, w_gate, w_up, conv_w, conv_b, w_down, g_ple, w_pg, w_pp)


def _head_blocks(w, width, n_heads):
    k = w.shape[0]
    w3 = w.reshape(k, n_heads, width)
    return jnp.pad(w3, ((0, 0), (0, 0), (0, LANES - width))).reshape(k, n_heads * LANES)


def _swap_rope_halves(a):
    half = MLA_ROPE // 2
    return jnp.concatenate([a[..., :MLA_NOPE], a[..., MLA_NOPE + half:MLA_QK],
                            a[..., MLA_NOPE:MLA_NOPE + half]], axis=-1)


def _lane_row(g, width=LANES):
    return jnp.pad(g, (0, width - g.shape[0])).reshape(1, width).astype(F32)


def _rope_tables(seq):
    half = MLA_ROPE // 2
    inv_freq = ROPE_THETA ** (-jnp.arange(half, dtype=F32) / half)
    ang = jnp.arange(seq, dtype=jnp.int32).astype(F32)[:, None] * inv_freq[None, :]
    cos, sin = jnp.cos(ang), jnp.sin(ang)
    ones = jnp.ones((seq, MLA_NOPE), F32)
    zeros_n = jnp.zeros((seq, MLA_NOPE), F32)
    zeros_p = jnp.zeros((seq, LANES - MLA_QK), F32)
    cos_t = jnp.concatenate([ones, cos, cos, zeros_p], axis=1)
    sin_t = jnp.concatenate([zeros_n, -sin, sin, zeros_p], axis=1)
    return cos_t, sin_t


def kernel(x, p, attn_norm_g, w_in, q_lat_norm_g, w_uq, kv_lat_norm_g, w_ukv, mla_q_norm_g, mla_k_norm_g,
           diff_q_norm_g, diff_k_norm_g, lambda_q1, lambda_k1, lambda_q2, lambda_k2, diff_out_norm_g,
           rel_bias, w_out, ffn_norm_g, w_gate, w_up, conv_w, conv_b, w_down, ple_norm_g, w_ple_gate,
           w_ple_proj):
    b, s, dm = x.shape
    depth = p.shape[0]
    assert s % TQ == 0 and s % IN_ROW_TILE == 0 and s % ROW_TILE == 0 and s % FFN_ROW_TILE == 0
    assert TQ % KSTEP == 0 and NEAR_KEYS % KSTEP == 0 and IN_ROW_TILE % TK == 0
    assert depth == 1

    cos_t, sin_t = _rope_tables(s)
    bias, mask = _bias_tiles(rel_bias.astype(F32))
    x2 = x.reshape(b * s, dm)

    for i in range(depth):
        wi = w_in[i]
        off_kr = MLA_Q_RANK + MLA_KV_RANK
        off_dq = off_kr + MLA_ROPE
        off_dv = off_dq + 2 * DIFF_HEADS * DIFF_V
        k_rope = wi[:, off_kr:off_dq]
        half = MLA_ROPE // 2
        k_rope_sw = jnp.concatenate([k_rope[:, half:], k_rope[:, :half]], axis=1)
        lane_pad = ((0, 0), (MLA_NOPE, LANES - MLA_QK))
        w_in_p = jnp.concatenate([wi[:, :off_kr], jnp.pad(k_rope, lane_pad), jnp.pad(k_rope_sw, lane_pad),
                                  wi[:, off_dq:off_dv]], axis=1).astype(BF16)
        w_dvt = wi[:, off_dv:].T.astype(BF16)
        w_uq_p = _head_blocks(w_uq[i], MLA_QK, MLA_HEADS).astype(BF16)
        w_uq_sw = _head_blocks(
            _swap_rope_halves(w_uq[i].reshape(MLA_Q_RANK, MLA_HEADS, MLA_QK)).reshape(MLA_Q_RANK, -1),
            MLA_QK, MLA_HEADS).astype(BF16)
        w_ukv3 = w_ukv[i].reshape(MLA_KV_RANK, MLA_HEADS, MLA_NOPE + MLA_V)
        w_uk_p = _head_blocks(w_ukv3[:, :, :MLA_NOPE].reshape(MLA_KV_RANK, -1), MLA_NOPE, MLA_HEADS).astype(BF16)
        w_uvt = w_ukv3[:, :, MLA_NOPE:].reshape(MLA_KV_RANK, MLA_HEADS * MLA_V).T.astype(BF16)
        gq, gk = mla_q_norm_g[i], mla_k_norm_g[i]
        gdq = jnp.tile(diff_q_norm_g[i], 2).reshape(1, LANES).astype(F32)
        gdk = jnp.tile(diff_k_norm_g[i], 2).reshape(1, LANES).astype(F32)

        qm, km, vmt, qd, kd, vdt = _in_proj(
            x2, cos_t, sin_t, attn_norm_g[i].reshape(1, dm), w_in_p, w_dvt, q_lat_norm_g[i].reshape(1, -1),
            w_uq_p, w_uq_sw, kv_lat_norm_g[i].reshape(1, -1), w_uk_p, w_uvt,
            _lane_row(gq), _lane_row(_swap_rope_halves(gq)), _lane_row(gk), _lane_row(_swap_rope_halves(gk)),
            gdq, gdk, b, s)

        y_mla = _mla_attention(qm.reshape(b, s, -1), km.reshape(b, s, -1), vmt, mask)
        lam_vecs = jnp.stack([lambda_q1[i], lambda_k1[i], lambda_q2[i], lambda_k2[i]]).astype(F32)
        y_diff = _diff_attention(rel_bias.astype(F32), qd.reshape(b, s, -1), kd.reshape(b, s, -1), vdt, bias,
                                 lam_vecs, diff_out_norm_g[i].reshape(1, DIFF_V).astype(F32))

        n_mla = MLA_HEADS * MLA_V
        x1, h2 = _out_proj(x2, y_mla.reshape(b * s, -1), y_diff.reshape(b * s, -1),
                           w_out[i][:n_mla].astype(BF16), w_out[i][n_mla:].astype(BF16),
                           ffn_norm_g[i].reshape(1, dm))
        x2 = _ffn_ple(x1, h2, p[i].reshape(b * s, -1), w_gate[i].astype(BF16), w_up[i].astype(BF16),
                      conv_w[i], conv_b[i].reshape(1, -1), w_down[i].astype(BF16),
                      ple_norm_g[i].reshape(1, dm), w_ple_gate[i].astype(BF16), w_ple_proj[i].astype(BF16), s)
    return x2.reshape(b, s, dm)
```

```python
import functools
import math

import jax
import jax.numpy as jnp
from jax import lax
from jax.experimental import pallas as pl
from jax.experimental.pallas import tpu as pltpu

F32 = jnp.float32
BF16 = jnp.bfloat16

LANES = 128
SUBLANES = 8
VMEM_LIMIT_BYTES = 56 * 1024 * 1024

CHUNK = 64
EPS = 1e-6
NEG_INF = -1e30
MLA_HEADS = 8
MLA_Q_RANK = 256
MLA_KV_RANK = 128
MLA_NOPE = 64
MLA_ROPE = 32
MLA_QK = MLA_NOPE + MLA_ROPE
MLA_V = 64
ROPE_THETA = 10000.0
DIFF_HEADS = 4
DIFF_QK = 64
DIFF_V = 2 * DIFF_QK
NUM_BUCKETS = 32
MAX_DISTANCE = 1024
CONV_WIDTH = 3
LAMBDA_INIT = 0.8 - 0.6 * math.exp(-0.3 * 0)
LOG2E = math.log2(math.e)

TQ = 512
TK = 256
KSPLIT = 2
KSTEP = TK * KSPLIT
NEAR_KEYS = 1024
MLA_HPS = 8
DIFF_HPS = 4
IN_ROW_TILE = 512
ROW_TILE = 512
FFN_ROW_TILE = 256
HALO_ROWS = SUBLANES

ZC_QLAT = 0
ZC_KVLAT = ZC_QLAT + MLA_Q_RANK
ZC_KR = ZC_KVLAT + MLA_KV_RANK
ZC_KRSW = ZC_KR + LANES
ZC_DQ = ZC_KRSW + LANES
ZC_DK = ZC_DQ + DIFF_HEADS * DIFF_V
ZC_END = ZC_DK + DIFF_HEADS * DIFF_V

NT_DIMS = (((1,), (1,)), ((), ()))
RESIDENT = pl.Buffered(1)


def _params(*sem):
    return pltpu.CompilerParams(dimension_semantics=sem, vmem_limit_bytes=VMEM_LIMIT_BYTES)


def _rms(x, width):
    return lax.rsqrt(jnp.sum(x * x, axis=-1, keepdims=True) * (1.0 / width) + EPS)


def _bias_kernel(tab_ref, bias_ref, mask_ref):
    t = TQ
    kk = lax.broadcasted_iota(jnp.int32, (t, t), 0)
    qq = lax.broadcasted_iota(jnp.int32, (t, t), 1)
    chunk_bits = CHUNK.bit_length() - 1
    q_chunk = lax.shift_right_logical(qq, chunk_bits)
    mask_ref[...] = jnp.where(lax.shift_right_logical(kk, chunk_bits) <= q_chunk, 0.0, NEG_INF).astype(F32)
    key_off = kk + pl.program_id(0) * t - NEAR_KEYS
    add_mask = jnp.where(lax.shift_right_arithmetic(key_off, chunk_bits) <= q_chunk, 0.0, NEG_INF).astype(F32)
    rel = key_off - qq
    nb = NUM_BUCKETS // 2
    max_exact = nb // 2
    sign_off = (rel > 0).astype(jnp.int32) * nb
    n = jnp.abs(rel)
    nf = jnp.maximum(n, 1).astype(F32)
    large = max_exact + (jnp.log(nf / max_exact) / math.log(MAX_DISTANCE / max_exact)
                         * (nb - max_exact)).astype(jnp.int32)
    large = jnp.minimum(large, nb - 1)
    bucket = sign_off + jnp.where(n < max_exact, n, large)
    for h in range(DIFF_HEADS):
        acc = jnp.zeros((t, t), F32)
        for b in range(NUM_BUCKETS):
            acc = jnp.where(bucket == b, tab_ref[b, h], acc)
        bias_ref[h] = acc * LOG2E + add_mask


def _bias_tiles(rel_bias):
    t = TQ
    n_blocks = (NEAR_KEYS + TQ) // t
    return pl.pallas_call(
        _bias_kernel,
        grid=(n_blocks,),
        in_specs=[pl.BlockSpec(memory_space=pltpu.SMEM)],
        out_specs=[pl.BlockSpec((DIFF_HEADS, t, t), lambda d: (0, d, 0)),
                   pl.BlockSpec((t, t), lambda d: (0, 0))],
        out_shape=[jax.ShapeDtypeStruct((DIFF_HEADS, n_blocks * t, t), F32),
                   jax.ShapeDtypeStruct((t, t), F32)],
        compiler_params=_params("arbitrary"),
        name="bias_tiles",
    )(rel_bias)


def _in_proj_kernel(x_ref, cos_ref, sin_ref, g_attn_ref, w_in_ref, w_dvt_ref, g_ql_ref, w_uq_ref, w_uqsw_ref,
                    g_kvl_ref, w_uk_ref, w_uvt_ref, gq_ref, gqsw_ref, gk_ref, gksw_ref,
                    gdq_ref, gdk_ref,
                    qm_ref, km_ref, vmt_ref, qd_ref, kd_ref, vdt_ref):
    x = x_ref[...]
    h = (x * _rms(x, x.shape[-1]) * g_attn_ref[...]).astype(BF16)
    z = jnp.dot(h, w_in_ref[...], preferred_element_type=F32)

    def store_transposed(dst_ref, w_t, act):
        v_t = lax.dot_general(w_t, act, NT_DIMS, preferred_element_type=F32).astype(BF16)
        for c in range(v_t.shape[1] // TK):
            dst_ref[0, c] = v_t[:, c * TK:(c + 1) * TK]

    store_transposed(vdt_ref, w_dvt_ref[...], h)

    q_lat = z[:, ZC_QLAT:ZC_KVLAT]
    kv_lat = z[:, ZC_KVLAT:ZC_KR]
    kr = z[:, ZC_KR:ZC_KRSW]
    krsw = z[:, ZC_KRSW:ZC_DQ]

    qln = (q_lat * _rms(q_lat, MLA_Q_RANK) * g_ql_ref[...]).astype(BF16)
    q = jnp.dot(qln, w_uq_ref[...], preferred_element_type=F32)
    qsw = jnp.dot(qln, w_uqsw_ref[...], preferred_element_type=F32)
    kvn = (kv_lat * _rms(kv_lat, MLA_KV_RANK) * g_kvl_ref[...]).astype(BF16)
    kn = jnp.dot(kvn, w_uk_ref[...], preferred_element_type=F32)
    store_transposed(vmt_ref, w_uvt_ref[...], kvn)

    cos = cos_ref[...]
    sin = sin_ref[...]
    q_scale = MLA_QK ** -0.5 * LOG2E
    q_cos = cos * (gq_ref[...] * q_scale)
    q_sin = sin * (gqsw_ref[...] * q_scale)
    k_cos = cos * gk_ref[...]
    k_sin = sin * gksw_ref[...]
    for hd in range(MLA_HEADS):
        sl = slice(hd * LANES, (hd + 1) * LANES)
        qh = q[:, sl]
        qm_ref[:, sl] = (_rms(qh, MLA_QK) * (qh * q_cos + qsw[:, sl] * q_sin)).astype(BF16)
        kh = kn[:, sl] + kr
        km_ref[:, sl] = (_rms(kh, MLA_QK) * (kh * k_cos + krsw * k_sin)).astype(BF16)

    lane = lax.broadcasted_iota(jnp.int32, (1, LANES), 1)
    first_map = lane < DIFF_QK
    d_scale = DIFF_QK ** -0.5 * LOG2E
    for hd in range(DIFF_HEADS):
        sl = slice(hd * LANES, (hd + 1) * LANES)
        for src, g_ref, dst, scale in ((ZC_DQ, gdq_ref, qd_ref, d_scale), (ZC_DK, gdk_ref, kd_ref, 1.0)):
            blk = z[:, src + hd * LANES: src + (hd + 1) * LANES]
            sq = blk * blk
            tot = jnp.sum(sq, axis=-1, keepdims=True)
            lo = jnp.sum(jnp.where(first_map, sq, 0.0), axis=-1, keepdims=True)
            ms = jnp.where(first_map, lo, tot - lo) * (1.0 / DIFF_QK)
            dst[:, sl] = (blk * lax.rsqrt(ms + EPS) * (g_ref[...] * scale)).astype(BF16)


def _in_proj(x2, cos_t, sin_t, g_attn, w_in_p, w_dvt, g_ql, w_uq_p, w_uq_sw, g_kvl, w_uk_p, w_uvt,
             gq, gqsw, gk, gksw, gdq, gdk, batch, seq):
    n, dm = x2.shape
    tm = IN_ROW_TILE
    tiles_per_seq = seq // tm
    row = lambda i: (i, 0)
    const = lambda i: (0, 0)
    pos = lambda i: (i % tiles_per_seq, 0)
    tile4 = lambda i: (i // tiles_per_seq, i % tiles_per_seq, 0, 0)
    kt = tm // TK

    def full(a):
        return pl.BlockSpec(a.shape, const)

    def rows_out(width):
        return pl.BlockSpec((tm, width), row), jax.ShapeDtypeStruct((n, width), BF16)

    def transposed_out(width):
        return (pl.BlockSpec((1, kt, width, TK), tile4),
                jax.ShapeDtypeStruct((batch, tiles_per_seq * kt, width, TK), BF16))

    outs = [rows_out(MLA_HEADS * LANES), rows_out(MLA_HEADS * LANES), transposed_out(MLA_HEADS * MLA_V),
            rows_out(DIFF_HEADS * DIFF_V), rows_out(DIFF_HEADS * DIFF_V), transposed_out(DIFF_HEADS * DIFF_V)]
    return pl.pallas_call(
        _in_proj_kernel,
        grid=(n // tm,),
        in_specs=[pl.BlockSpec((tm, dm), row), pl.BlockSpec((tm, LANES), pos), pl.BlockSpec((tm, LANES), pos),
                  full(g_attn), full(w_in_p), full(w_dvt), full(g_ql), full(w_uq_p), full(w_uq_sw),
                  full(g_kvl), full(w_uk_p), full(w_uvt), full(gq), full(gqsw), full(gk), full(gksw),
                  full(gdq), full(gdk)],
        out_specs=[o[0] for o in outs],
        out_shape=[o[1] for o in outs],
        compiler_params=_params("parallel"),
        name="in_proj",
    )(x2, cos_t, sin_t, g_attn, w_in_p, w_dvt, g_ql, w_uq_p, w_uq_sw, g_kvl, w_uk_p, w_uvt,
      gq, gqsw, gk, gksw, gdq, gdk)


def _softmax_step(s_tiles, shift, m, l, acc, vt_tiles):
    col_max = functools.reduce(jnp.maximum, [jnp.max(s, axis=0, keepdims=True) for s in s_tiles])
    if shift is not None:
        col_max = col_max + shift
    m_new = jnp.maximum(m, col_max)
    alpha = jnp.exp2(m - m_new)
    m_sub = m_new if shift is None else m_new - shift
    l_new = alpha * l
    acc_new = alpha * acc
    for s, vt in zip(s_tiles, vt_tiles):
        p = jnp.exp2(s - m_sub)
        l_new = l_new + jnp.sum(p, axis=0, keepdims=True)
        acc_new = acc_new + jnp.dot(vt, p.astype(BF16), preferred_element_type=F32)
    return m_new, l_new, acc_new


def _key_rows(step_idx, c):
    return pl.ds(pl.multiple_of(step_idx * KSTEP + c * TK, TK), TK)


def _init_state(tq, dv):
    return (jnp.full((1, tq), NEG_INF, F32), jnp.zeros((1, tq), F32), jnp.zeros((dv, tq), F32))


def _run_chains(chains, loops, n_all, s_scr, m_scr, l_scr, acc_scr):
    for ci, (qk, _) in enumerate(chains):
        m_scr[ci] = jnp.full(m_scr.shape[1:], NEG_INF, F32)
        l_scr[ci] = jnp.zeros(l_scr.shape[1:], F32)
        acc_scr[ci] = jnp.zeros(acc_scr.shape[1:], F32)
        for c, s in enumerate(qk(0)):
            s_scr[ci, c] = s

    def step(j, add, shift):
        for ci, (qk, vt) in enumerate(chains):
            s_tiles = [s_scr[ci, c] for c in range(KSPLIT)]
            if add is not None:
                s_tiles = [s + add(ci, j, c) for c, s in enumerate(s_tiles)]
            m_new, l_new, acc_new = _softmax_step(s_tiles, None if shift is None else shift(ci),
                                                  m_scr[ci], l_scr[ci], acc_scr[ci], vt(j))
            m_scr[ci] = m_new
            l_scr[ci] = l_new
            acc_scr[ci] = acc_new
            for c, s in enumerate(qk(jnp.minimum(j + 1, n_all - 1))):
                s_scr[ci, c] = s

    for start, stop, add, shift in loops:
        def body(j, carry, add=add, shift=shift):
            step(j, add, shift)
            return carry
        lax.fori_loop(start, stop, body, 0)


def _chain_scratch(n_chains, dv):
    return [pltpu.VMEM((n_chains, KSPLIT, TK, TQ), F32), pltpu.VMEM((n_chains, 1, TQ), F32),
            pltpu.VMEM((n_chains, 1, TQ), F32), pltpu.VMEM((n_chains, dv, TQ), F32)]


def _mla_kernel(q_ref, k_ref, vt_ref, mask_ref, o_ref, s_scr, m_scr, l_scr, acc_scr):
    qi = pl.program_id(2)
    steps_per_q = TQ // KSTEP
    n_all = (qi + 1) * steps_per_q

    def chain(hh):
        q = q_ref[0, :, hh * LANES:(hh + 1) * LANES]

        def qk(j):
            return [lax.dot_general(k_ref[0, _key_rows(j, c), hh * LANES:(hh + 1) * LANES], q, NT_DIMS,
                                    preferred_element_type=F32) for c in range(KSPLIT)]

        def vt(j):
            return [vt_ref[0, j * KSPLIT + c, hh * MLA_V:(hh + 1) * MLA_V, :] for c in range(KSPLIT)]

        return qk, vt

    def mask(ci, j, c):
        return mask_ref[pl.ds(pl.multiple_of((j * KSPLIT + c) * TK - qi * TQ, TK), TK), :]

    _run_chains([chain(hh) for hh in range(MLA_HPS)],
                [(0, qi * steps_per_q, None, None), (qi * steps_per_q, n_all, mask, None)],
                n_all, s_scr, m_scr, l_scr, acc_scr)
    o_t = jnp.concatenate([acc_scr[hh] / l_scr[hh] for hh in range(MLA_HPS)], axis=0)
    o_ref[0] = o_t.T.astype(o_ref.dtype)


def _mla_attention(qm, km, vmt, mask):
    b, s, _ = qm.shape
    t = TQ
    pairs = MLA_HEADS // MLA_HPS
    return pl.pallas_call(
        _mla_kernel,
        grid=(b, pairs, s // t),
        in_specs=[pl.BlockSpec((1, t, MLA_HPS * LANES), lambda bi, hp, qi: (bi, qi, hp)),
                  pl.BlockSpec((1, s, MLA_HPS * LANES), lambda bi, hp, qi: (bi, 0, hp), pipeline_mode=RESIDENT),
                  pl.BlockSpec((1, s // TK, MLA_HPS * MLA_V, TK), lambda bi, hp, qi: (bi, 0, hp, 0),
                               pipeline_mode=RESIDENT),
                  pl.BlockSpec((t, t), lambda bi, hp, qi: (0, 0))],
        out_specs=pl.BlockSpec((1, t, MLA_HPS * MLA_V), lambda bi, hp, qi: (bi, qi, hp)),
        out_shape=jax.ShapeDtypeStruct((b, s, MLA_HEADS * MLA_V), BF16),
        scratch_shapes=_chain_scratch(MLA_HPS, MLA_V),
        compiler_params=_params("parallel", "parallel", "arbitrary"),
        name="mla_attn",
    )(qm, km, vmt, mask)


def _diff_kernel(tab_ref, q_ref, k_ref, vt_ref, bias_ref, lam_ref, g_out_ref, o_ref,
                 s_scr, m_scr, l_scr, acc_scr):
    hp = pl.program_id(1)
    qi = pl.program_id(2)
    lane = lax.broadcasted_iota(jnp.int32, (1, LANES), 1)
    far_bias = [tab_ref[NUM_BUCKETS // 2 - 1, hp * DIFF_HPS + hh] * LOG2E for hh in range(DIFF_HPS)]
    strip_start = qi * TQ - NEAR_KEYS

    def chain(hh, mp):
        q = q_ref[0, :, hh * LANES:(hh + 1) * LANES]
        in_map = (lane >= DIFF_QK) if mp else (lane < DIFF_QK)
        q = jnp.where(in_map, q, jnp.zeros_like(q))

        def qk(j):
            return [lax.dot_general(k_ref[0, _key_rows(j, c), hh * LANES:(hh + 1) * LANES], q, NT_DIMS,
                                    preferred_element_type=F32) for c in range(KSPLIT)]

        def vt(j):
            return [vt_ref[0, j * KSPLIT + c, hh * DIFF_V:(hh + 1) * DIFF_V, :] for c in range(KSPLIT)]

        return qk, vt

    def near_bias(ci, j, c):
        off = pl.multiple_of((j * KSPLIT + c) * TK - strip_start, TK)
        return bias_ref[ci // 2, pl.ds(off, TK), :]

    n_far = jnp.maximum(qi * (TQ // KSTEP) - NEAR_KEYS // KSTEP, 0)
    n_all = (qi + 1) * (TQ // KSTEP)
    _run_chains([chain(hh, mp) for hh in range(DIFF_HPS) for mp in range(2)],
                [(0, n_far, None, lambda ci: far_bias[ci // 2]), (n_far, n_all, near_bias, None)],
                n_all, s_scr, m_scr, l_scr, acc_scr)

    lv = lam_ref[...]
    lam = (jnp.exp(jnp.sum(lv[0:1] * lv[1:2], axis=-1, keepdims=True))
           - jnp.exp(jnp.sum(lv[2:3] * lv[3:4], axis=-1, keepdims=True)) + LAMBDA_INIT)
    for hh in range(DIFF_HPS):
        a0, l0, a1, l1 = acc_scr[2 * hh], l_scr[2 * hh], acc_scr[2 * hh + 1], l_scr[2 * hh + 1]
        o_t = a0 / l0 - lam * (a1 / l1)
        ms = jnp.sum(o_t * o_t, axis=0, keepdims=True) * (1.0 / DIFF_V)
        o_t = o_t * lax.rsqrt(ms + EPS)
        o_ref[0, :, hh * LANES:(hh + 1) * LANES] = (
            o_t.T * g_out_ref[...] * (1.0 - LAMBDA_INIT)).astype(o_ref.dtype)


def _diff_attention(rel_bias, qd, kd, vdt, bias, lam_vecs, g_out):
    b, s, _ = qd.shape
    t = TQ
    return pl.pallas_call(
        _diff_kernel,
        grid=(b, DIFF_HEADS // DIFF_HPS, s // t),
        in_specs=[pl.BlockSpec(memory_space=pltpu.SMEM),
                  pl.BlockSpec((1, t, DIFF_HPS * LANES), lambda bi, hd, qi: (bi, qi, hd)),
                  pl.BlockSpec((1, s, DIFF_HPS * LANES), lambda bi, hd, qi: (bi, 0, hd), pipeline_mode=RESIDENT),
                  pl.BlockSpec((1, s // TK, DIFF_HPS * DIFF_V, TK), lambda bi, hd, qi: (bi, 0, hd, 0),
                               pipeline_mode=RESIDENT),
                  pl.BlockSpec((DIFF_HPS, NEAR_KEYS + t, t), lambda bi, hd, qi: (hd, 0, 0),
                               pipeline_mode=RESIDENT),
                  pl.BlockSpec(lam_vecs.shape, lambda bi, hd, qi: (0, 0)),
                  pl.BlockSpec(g_out.shape, lambda bi, hd, qi: (0, 0))],
        out_specs=pl.BlockSpec((1, t, DIFF_HPS * LANES), lambda bi, hd, qi: (bi, qi, hd)),
        out_shape=jax.ShapeDtypeStruct((b, s, DIFF_HEADS * DIFF_V), BF16),
        scratch_shapes=_chain_scratch(2 * DIFF_HPS, DIFF_V),
        compiler_params=_params("parallel", "parallel", "arbitrary"),
        name="diff_attn",
    )(rel_bias, qd, kd, vdt, bias, lam_vecs, g_out)


def _out_proj_kernel(x_ref, ym_ref, yd_ref, wm_ref, wd_ref, g_ref, x1_ref, h2_ref):
    x1 = (x_ref[...]
          + jnp.dot(ym_ref[...], wm_ref[...], preferred_element_type=F32)
          + jnp.dot(yd_ref[...], wd_ref[...], preferred_element_type=F32))
    x1_ref[...] = x1
    h2_ref[...] = (x1 * _rms(x1, x1.shape[-1]) * g_ref[...]).astype(BF16)


def _out_proj(x2, ym, yd, w_out_m, w_out_d, g_ffn):
    n, dm = x2.shape
    tm = ROW_TILE
    row = lambda i: (i, 0)
    const = lambda i: (0, 0)
    return pl.pallas_call(
        _out_proj_kernel,
        grid=(n // tm,),
        in_specs=[pl.BlockSpec((tm, dm), row), pl.BlockSpec((tm, ym.shape[1]), row),
                  pl.BlockSpec((tm, yd.shape[1]), row), pl.BlockSpec(w_out_m.shape, const),
                  pl.BlockSpec(w_out_d.shape, const), pl.BlockSpec(g_ffn.shape, const)],
        out_specs=[pl.BlockSpec((tm, dm), row), pl.BlockSpec((tm, dm), row)],
        out_shape=[jax.ShapeDtypeStruct((n, dm), F32), jax.ShapeDtypeStruct((n, dm), BF16)],
        compiler_params=_params("parallel"),
        name="out_proj",
    )(x2, ym, yd, w_out_m, w_out_d, g_ffn)


def _ffn_kernel(tiles_per_seq, x1_ref, h2_ref, halo_ref, p_ref, wg_ref, wu_ref, cw_ref, cb_ref, wd_ref,
                g_ple_ref, wpg_ref, wpp_ref, o_ref, g_scr):
    tm = x1_ref.shape[0]
    h2 = h2_ref[...]
    wg = wg_ref[...]
    keep = (pl.program_id(0) % tiles_per_seq != 0).astype(F32)
    g_scr[0:HALO_ROWS, :] = jnp.dot(halo_ref[...], wg, preferred_element_type=F32) * keep
    g_scr[HALO_ROWS:, :] = jnp.dot(h2, wg, preferred_element_type=F32)
    conv = cb_ref[...]
    for j in range(CONV_WIDTH):
        start = HALO_ROWS - (CONV_WIDTH - 1) + j
        conv = conv + g_scr[start:start + tm, :] * cw_ref[j:j + 1, :]
    up = jnp.dot(h2, wu_ref[...], preferred_element_type=F32)
    act = (conv * jax.nn.sigmoid(conv) * up).astype(BF16)
    x2 = x1_ref[...] + jnp.dot(act, wd_ref[...], preferred_element_type=F32)
    hn = (x2 * _rms(x2, x2.shape[-1]) * g_ple_ref[...]).astype(BF16)
    gate = jax.nn.sigmoid(jnp.dot(hn, wpg_ref[...], preferred_element_type=F32))
    proj = jnp.dot(p_ref[...].astype(BF16), wpp_ref[...], preferred_element_type=F32)
    o_ref[...] = x2 + gate * proj


def _ffn_ple(x1, h2, p2, w_gate, w_up, conv_w, conv_b, w_down, g_ple, w_pg, w_pp, seq):
    n, dm = x1.shape
    tm = min(FFN_ROW_TILE, seq)
    d_ff = w_gate.shape[1]
    row = lambda i: (i, 0)
    const = lambda i: (0, 0)
    halo = lambda i: (jnp.maximum(i * (tm // HALO_ROWS) - 1, 0), 0)

    def full(a):
        return pl.BlockSpec(a.shape, const)

    return pl.pallas_call(
        functools.partial(_ffn_kernel, seq // tm),
        grid=(n // tm,),
        in_specs=[pl.BlockSpec((tm, dm), row), pl.BlockSpec((tm, dm), row),
                  pl.BlockSpec((HALO_ROWS, dm), halo), pl.BlockSpec((tm, p2.shape[1]), row),
                  full(w_gate), full(w_up), full(conv_w), full(conv_b), full(w_down),
                  full(g_ple), full(w_pg), full(w_pp)],
        out_specs=pl.BlockSpec((tm, dm), row),
        out_shape=jax.ShapeDtypeStruct((n, dm), F32),
        scratch_shapes=[pltpu.VMEM((tm + HALO_ROWS, d_ff), F32)],
        compiler_params=_params("parallel"),
        name="ffn_ple",
    )(x1, h2, h2, p2, w_gate, w_up, conv_w, conv_b, w_down, g_ple, w_pg, w_pp)


def _head_blocks(w, width, n_heads):
    k = w.shape[0]
    w3 = w.reshape(k, n_heads, width)
    return jnp.pad(w3, ((0, 0), (0, 0), (0, LANES - width))).reshape(k, n_heads * LANES)


def _swap_rope_halves(a):
    half = MLA_ROPE // 2
    return jnp.concatenate([a[..., :MLA_NOPE], a[..., MLA_NOPE + half:MLA_QK],
                            a[..., MLA_NOPE:MLA_NOPE + half]], axis=-1)


def _lane_row(g, width=LANES):
    return jnp.pad(g, (0, width - g.shape[0])).reshape(1, width).astype(F32)


def _rope_tables(seq):
    half = MLA_ROPE // 2
    inv_freq = ROPE_THETA ** (-jnp.arange(half, dtype=F32) / half)
    ang = jnp.arange(seq, dtype=jnp.int32).astype(F32)[:, None] * inv_freq[None, :]
    cos, sin = jnp.cos(ang), jnp.sin(ang)
    ones = jnp.ones((seq, MLA_NOPE), F32)
    zeros_n = jnp.zeros((seq, MLA_NOPE), F32)
    zeros_p = jnp.zeros((seq, LANES - MLA_QK), F32)
    cos_t = jnp.concatenate([ones, cos, cos, zeros_p], axis=1)
    sin_t = jnp.concatenate([zeros_n, -sin, sin, zeros_p], axis=1)
    return cos_t, sin_t


def kernel(x, p, attn_norm_g, w_in, q_lat_norm_g, w_uq, kv_lat_norm_g, w_ukv, mla_q_norm_g, mla_k_norm_g,
           diff_q_norm_g, diff_k_norm_g, lambda_q1, lambda_k1, lambda_q2, lambda_k2, diff_out_norm_g,
           rel_bias, w_out, ffn_norm_g, w_gate, w_up, conv_w, conv_b, w_down, ple_norm_g, w_ple_gate,
           w_ple_proj):
    b, s, dm = x.shape
    depth = p.shape[0]
    assert s % TQ == 0 and s % IN_ROW_TILE == 0 and s % ROW_TILE == 0 and s % FFN_ROW_TILE == 0
    assert TQ % KSTEP == 0 and NEAR_KEYS % KSTEP == 0 and IN_ROW_TILE % TK == 0
    assert depth == 1

    cos_t, sin_t = _rope_tables(s)
    bias, mask = _bias_tiles(rel_bias.astype(F32))
    x2 = x.reshape(b * s, dm)

    for i in range(depth):
        wi = w_in[i]
        off_kr = MLA_Q_RANK + MLA_KV_RANK
        off_dq = off_kr + MLA_ROPE
        off_dv = off_dq + 2 * DIFF_HEADS * DIFF_V
        k_rope = wi[:, off_kr:off_dq]
        half = MLA_ROPE // 2
        k_rope_sw = jnp.concatenate([k_rope[:, half:], k_rope[:, :half]], axis=1)
        lane_pad = ((0, 0), (MLA_NOPE, LANES - MLA_QK))
        w_in_p = jnp.concatenate([wi[:, :off_kr], jnp.pad(k_rope, lane_pad), jnp.pad(k_rope_sw, lane_pad),
                                  wi[:, off_dq:off_dv]], axis=1).astype(BF16)
        w_dvt = wi[:, off_dv:].T.astype(BF16)
        w_uq_p = _head_blocks(w_uq[i], MLA_QK, MLA_HEADS).astype(BF16)
        w_uq_sw = _head_blocks(
            _swap_rope_halves(w_uq[i].reshape(MLA_Q_RANK, MLA_HEADS, MLA_QK)).reshape(MLA_Q_RANK, -1),
            MLA_QK, MLA_HEADS).astype(BF16)
        w_ukv3 = w_ukv[i].reshape(MLA_KV_RANK, MLA_HEADS, MLA_NOPE + MLA_V)
        w_uk_p = _head_blocks(w_ukv3[:, :, :MLA_NOPE].reshape(MLA_KV_RANK, -1), MLA_NOPE, MLA_HEADS).astype(BF16)
        w_uvt = w_ukv3[:, :, MLA_NOPE:].reshape(MLA_KV_RANK, MLA_HEADS * MLA_V).T.astype(BF16)
        gq, gk = mla_q_norm_g[i], mla_k_norm_g[i]
        gdq = jnp.tile(diff_q_norm_g[i], 2).reshape(1, LANES).astype(F32)
        gdk = jnp.tile(diff_k_norm_g[i], 2).reshape(1, LANES).astype(F32)

        qm, km, vmt, qd, kd, vdt = _in_proj(
            x2, cos_t, sin_t, attn_norm_g[i].reshape(1, dm), w_in_p, w_dvt, q_lat_norm_g[i].reshape(1, -1),
            w_uq_p, w_uq_sw, kv_lat_norm_g[i].reshape(1, -1), w_uk_p, w_uvt,
            _lane_row(gq), _lane_row(_swap_rope_halves(gq)), _lane_row(gk), _lane_row(_swap_rope_halves(gk)),
            gdq, gdk, b, s)

        y_mla = _mla_attention(qm.reshape(b, s, -1), km.reshape(b, s, -1), vmt, mask)
        lam_vecs = jnp.stack([lambda_q1[i], lambda_k1[i], lambda_q2[i], lambda_k2[i]]).astype(F32)
        y_diff = _diff_attention(rel_bias.astype(F32), qd.reshape(b, s, -1), kd.reshape(b, s, -1), vdt, bias,
                                 lam_vecs, diff_out_norm_g[i].reshape(1, DIFF_V).astype(F32))

        n_mla = MLA_HEADS * MLA_V
        x1, h2 = _out_proj(x2, y_mla.reshape(b * s, -1), y_diff.reshape(b * s, -1),
                           w_out[i][:n_mla].astype(BF16), w_out[i][n_mla:].astype(BF16),
                           ffn_norm_g[i].reshape(1, dm))
        x2 = _ffn_ple(x1, h2, p[i].reshape(b * s, -1), w_gate[i].astype(BF16), w_up[i].astype(BF16),
                      conv_w[i], conv_b[i].reshape(1, -1), w_down[i].astype(BF16),
                      ple_norm_g[i].reshape(1, dm), w_ple_gate[i].astype(BF16), w_ple_proj[i].astype(BF16), s)
    return x2.reshape(b, s, dm)
```

```python
import functools
import math

import jax
import jax.numpy as jnp
from jax import lax
from jax.experimental import pallas as pl
from jax.experimental.pallas import tpu as pltpu

F32 = jnp.float32
BF16 = jnp.bfloat16

LANES = 128
SUBLANES = 8
VMEM_LIMIT_BYTES = 56 * 1024 * 1024

CHUNK = 64
EPS = 1e-6
NEG_INF = -1e30
MLA_HEADS = 8
MLA_Q_RANK = 256
MLA_KV_RANK = 128
MLA_NOPE = 64
MLA_ROPE = 32
MLA_QK = MLA_NOPE + MLA_ROPE
MLA_V = 64
ROPE_THETA = 10000.0
DIFF_HEADS = 4
DIFF_QK = 64
DIFF_V = 2 * DIFF_QK
NUM_BUCKETS = 32
MAX_DISTANCE = 1024
CONV_WIDTH = 3
LAMBDA_INIT = 0.8 - 0.6 * math.exp(-0.3 * 0)
LOG2E = math.log2(math.e)

TQ = 512
TK = 256
KSPLIT = 2
KSTEP = TK * KSPLIT
SUM_ROWS = 16
NEAR_KEYS = 1024
MLA_HPS = 8
DIFF_HPS = 4
IN_ROW_TILE = 512
FFN_ROW_TILE = 256
HALO_ROWS = 2 * SUBLANES

ZC_QLAT = 0
ZC_KVLAT = ZC_QLAT + MLA_Q_RANK
ZC_KR = ZC_KVLAT + MLA_KV_RANK
ZC_KRSW = ZC_KR + LANES
ZC_DQ = ZC_KRSW + LANES
ZC_DK = ZC_DQ + DIFF_HEADS * DIFF_V
ZC_END = ZC_DK + DIFF_HEADS * DIFF_V

NT_DIMS = (((1,), (1,)), ((), ()))
RESIDENT = pl.Buffered(1)


def _params(*sem):
    return pltpu.CompilerParams(dimension_semantics=sem, vmem_limit_bytes=VMEM_LIMIT_BYTES)


def _rms(x, width):
    return lax.rsqrt(jnp.sum(x * x, axis=-1, keepdims=True) * (1.0 / width) + EPS)


def _bias_kernel(tab_ref, bias_ref, mask_ref):
    t = TQ
    kk = lax.broadcasted_iota(jnp.int32, (t, t), 0)
    qq = lax.broadcasted_iota(jnp.int32, (t, t), 1)
    chunk_bits = CHUNK.bit_length() - 1
    q_chunk = lax.shift_right_logical(qq, chunk_bits)
    mask_ref[...] = jnp.where(lax.shift_right_logical(kk, chunk_bits) <= q_chunk, 0.0, NEG_INF).astype(F32)
    key_off = kk + pl.program_id(0) * t - NEAR_KEYS
    add_mask = jnp.where(lax.shift_right_arithmetic(key_off, chunk_bits) <= q_chunk, 0.0, NEG_INF).astype(F32)
    rel = key_off - qq
    nb = NUM_BUCKETS // 2
    max_exact = nb // 2
    sign_off = (rel > 0).astype(jnp.int32) * nb
    n = jnp.abs(rel)
    nf = jnp.maximum(n, 1).astype(F32)
    large = max_exact + (jnp.log(nf / max_exact) / math.log(MAX_DISTANCE / max_exact)
                         * (nb - max_exact)).astype(jnp.int32)
    large = jnp.minimum(large, nb - 1)
    bucket = sign_off + jnp.where(n < max_exact, n, large)
    for h in range(DIFF_HEADS):
        acc = jnp.zeros((t, t), F32)
        for b in range(NUM_BUCKETS):
            acc = jnp.where(bucket == b, tab_ref[b, h], acc)
        bias_ref[h] = acc * LOG2E + add_mask


def _bias_tiles(rel_bias):
    t = TQ
    n_blocks = (NEAR_KEYS + TQ) // t
    return pl.pallas_call(
        _bias_kernel,
        grid=(n_blocks,),
        in_specs=[pl.BlockSpec(memory_space=pltpu.SMEM)],
        out_specs=[pl.BlockSpec((DIFF_HEADS, t, t), lambda d: (0, d, 0)),
                   pl.BlockSpec((t, t), lambda d: (0, 0))],
        out_shape=[jax.ShapeDtypeStruct((DIFF_HEADS, n_blocks * t, t), F32),
                   jax.ShapeDtypeStruct((t, t), F32)],
        compiler_params=_params("arbitrary"),
        name="bias_tiles",
    )(rel_bias)


def _in_proj_kernel(x_ref, cos_ref, sin_ref, g_attn_ref, w_in_ref, w_dvt_ref, g_ql_ref, w_uq_ref, w_uqsw_ref,
                    g_kvl_ref, w_uk_ref, w_uvt_ref, gq_ref, gqsw_ref, gk_ref, gksw_ref,
                    gdq_ref, gdk_ref,
                    qm_ref, km_ref, vmt_ref, qd_ref, kd_ref, vdt_ref):
    x = x_ref[...]
    h = (x * _rms(x, x.shape[-1]) * g_attn_ref[...]).astype(BF16)
    z = jnp.dot(h, w_in_ref[...], preferred_element_type=F32)

    def store_transposed(dst_ref, w_t, act):
        v_t = lax.dot_general(w_t, act, NT_DIMS, preferred_element_type=F32).astype(BF16)
        for c in range(v_t.shape[1] // TK):
            dst_ref[0, c] = v_t[:, c * TK:(c + 1) * TK]

    store_transposed(vdt_ref, w_dvt_ref[...], h)

    q_lat = z[:, ZC_QLAT:ZC_KVLAT]
    kv_lat = z[:, ZC_KVLAT:ZC_KR]
    kr = z[:, ZC_KR:ZC_KRSW]
    krsw = z[:, ZC_KRSW:ZC_DQ]

    qln = (q_lat * _rms(q_lat, MLA_Q_RANK) * g_ql_ref[...]).astype(BF16)
    q = jnp.dot(qln, w_uq_ref[...], preferred_element_type=F32)
    qsw = jnp.dot(qln, w_uqsw_ref[...], preferred_element_type=F32)
    kvn = (kv_lat * _rms(kv_lat, MLA_KV_RANK) * g_kvl_ref[...]).astype(BF16)
    kn = jnp.dot(kvn, w_uk_ref[...], preferred_element_type=F32)
    store_transposed(vmt_ref, w_uvt_ref[...], kvn)

    cos = cos_ref[...]
    sin = sin_ref[...]
    q_scale = MLA_QK ** -0.5 * LOG2E
    q_cos = cos * (gq_ref[...] * q_scale)
    q_sin = sin * (gqsw_ref[...] * q_scale)
    k_cos = cos * gk_ref[...]
    k_sin = sin * gksw_ref[...]
    for hd in range(MLA_HEADS):
        sl = slice(hd * LANES, (hd + 1) * LANES)
        qh = q[:, sl]
        qm_ref[:, sl] = (_rms(qh, MLA_QK) * (qh * q_cos + qsw[:, sl] * q_sin)).astype(BF16)
        kh = kn[:, sl] + kr
        km_ref[:, sl] = (_rms(kh, MLA_QK) * (kh * k_cos + krsw * k_sin)).astype(BF16)

    lane = lax.broadcasted_iota(jnp.int32, (1, LANES), 1)
    first_map = lane < DIFF_QK
    d_scale = DIFF_QK ** -0.5 * LOG2E
    for hd in range(DIFF_HEADS):
        sl = slice(hd * LANES, (hd + 1) * LANES)
        for src, g_ref, dst, scale in ((ZC_DQ, gdq_ref, qd_ref, d_scale), (ZC_DK, gdk_ref, kd_ref, 1.0)):
            blk = z[:, src + hd * LANES: src + (hd + 1) * LANES]
            sq = blk * blk
            tot = jnp.sum(sq, axis=-1, keepdims=True)
            lo = jnp.sum(jnp.where(first_map, sq, 0.0), axis=-1, keepdims=True)
            ms = jnp.where(first_map, lo, tot - lo) * (1.0 / DIFF_QK)
            dst[:, sl] = (blk * lax.rsqrt(ms + EPS) * (g_ref[...] * scale)).astype(BF16)


def _in_proj(x2, cos_t, sin_t, g_attn, w_in_p, w_dvt, g_ql, w_uq_p, w_uq_sw, g_kvl, w_uk_p, w_uvt,
             gq, gqsw, gk, gksw, gdq, gdk, batch, seq):
    n, dm = x2.shape
    tm = IN_ROW_TILE
    tiles_per_seq = seq // tm
    row = lambda i: (i, 0)
    const = lambda i: (0, 0)
    pos = lambda i: (i % tiles_per_seq, 0)
    tile4 = lambda i: (i // tiles_per_seq, i % tiles_per_seq, 0, 0)
    kt = tm // TK

    def full(a):
        return pl.BlockSpec(a.shape, const)

    def rows_out(width):
        return pl.BlockSpec((tm, width), row), jax.ShapeDtypeStruct((n, width), BF16)

    def transposed_out(width):
        return (pl.BlockSpec((1, kt, width, TK), tile4),
                jax.ShapeDtypeStruct((batch, tiles_per_seq * kt, width, TK), BF16))

    outs = [rows_out(MLA_HEADS * LANES), rows_out(MLA_HEADS * LANES), transposed_out(MLA_HEADS * MLA_V),
            rows_out(DIFF_HEADS * DIFF_V), rows_out(DIFF_HEADS * DIFF_V), transposed_out(DIFF_HEADS * DIFF_V)]
    return pl.pallas_call(
        _in_proj_kernel,
        grid=(n // tm,),
        in_specs=[pl.BlockSpec((tm, dm), row), pl.BlockSpec((tm, LANES), pos), pl.BlockSpec((tm, LANES), pos),
                  full(g_attn), full(w_in_p), full(w_dvt), full(g_ql), full(w_uq_p), full(w_uq_sw),
                  full(g_kvl), full(w_uk_p), full(w_uvt), full(gq), full(gqsw), full(gk), full(gksw),
                  full(gdq), full(gdk)],
        out_specs=[o[0] for o in outs],
        out_shape=[o[1] for o in outs],
        compiler_params=_params("parallel"),
        name="in_proj",
    )(x2, cos_t, sin_t, g_attn, w_in_p, w_dvt, g_ql, w_uq_p, w_uq_sw, g_kvl, w_uk_p, w_uvt,
      gq, gqsw, gk, gksw, gdq, gdk)


def _col_max(s_tiles):
    return functools.reduce(jnp.maximum, [jnp.max(s, axis=0, keepdims=True) for s in s_tiles])


def _key_rows(step_idx, c):
    return pl.ds(pl.multiple_of(step_idx * KSTEP + c * TK, TK), TK)


def _run_chains(chains, n_plain, n_all, s_scr, cm_scr, sh_scr, m_scr, acc_scr):
    row = lax.broadcasted_iota(jnp.int32, (SUM_ROWS, TK), 0)
    ones_rows = jnp.where(row == 0, 1.0, 0.0).astype(BF16)
    use_shift = any(shift is not None for (_, _, _, shift) in chains)

    def produce(ci, j, decorated):
        qk, _, add, shift = chains[ci]
        s_tiles = qk(j)
        if decorated:
            s_tiles = [s + add(j, c) for c, s in enumerate(s_tiles)]
        for c, s in enumerate(s_tiles):
            s_scr[ci, c] = s
        owed = shift if (shift is not None and not decorated) else 0.0
        cm_scr[ci] = functools.reduce(jnp.maximum, [jnp.max(s, axis=0, keepdims=True) for s in s_tiles]) + owed
        if use_shift:
            sh_scr[ci] = jnp.zeros(sh_scr.shape[1:], F32) + owed

    def consume(ci, j):
        _, vt, _, _ = chains[ci]
        m = m_scr[ci]
        m_new = jnp.maximum(m, cm_scr[ci])
        alpha = jnp.exp2(m - m_new)
        m_sub = m_new - sh_scr[ci] if use_shift else m_new
        acc = alpha * acc_scr[ci]
        for c, v_t in enumerate(vt(j)):
            p = jnp.exp2(s_scr[ci, c] - m_sub).astype(BF16)
            acc = acc + jnp.dot(jnp.concatenate([v_t, ones_rows], axis=0), p, preferred_element_type=F32)
        m_scr[ci] = m_new
        acc_scr[ci] = acc

    n_chains = len(chains)
    for ci in range(n_chains):
        m_scr[ci] = jnp.full(m_scr.shape[1:], NEG_INF, F32)
        acc_scr[ci] = jnp.zeros(acc_scr.shape[1:], F32)
    for decorated in (False, True):
        @pl.when((n_plain == 0) == decorated)
        def _():
            for ci in range(n_chains):
                produce(ci, 0, decorated)

    def run(start, stop, decorated):
        def body(j, carry):
            for ci in range(n_chains):
                consume(ci, j)
                produce(ci, jnp.minimum(j + 1, n_all - 1), decorated)
            return carry
        lax.fori_loop(start, stop, body, 0)

    switch = jnp.maximum(n_plain - 1, 0)
    run(0, switch, False)
    run(switch, n_all, True)


def _chain_scratch(n_chains, dv):
    stat = pltpu.VMEM((n_chains, 1, TQ), F32)
    return [pltpu.VMEM((n_chains, KSPLIT, TK, TQ), F32), stat, stat, stat,
            pltpu.VMEM((n_chains, dv + SUM_ROWS, TQ), F32)]


def _mla_kernel(q_ref, k_ref, vt_ref, mask_ref, o_ref, s_scr, cm_scr, sh_scr, m_scr, acc_scr):
    qi = pl.program_id(2)
    steps_per_q = TQ // KSTEP
    n_all = (qi + 1) * steps_per_q

    def mask(j, c):
        return mask_ref[pl.ds(pl.multiple_of((j * KSPLIT + c) * TK - qi * TQ, TK), TK), :]

    def chain(hh):
        q = q_ref[0, :, hh * LANES:(hh + 1) * LANES]

        def qk(j):
            return [lax.dot_general(k_ref[0, _key_rows(j, c), hh * LANES:(hh + 1) * LANES], q, NT_DIMS,
                                    preferred_element_type=F32) for c in range(KSPLIT)]

        def vt(j):
            return [vt_ref[0, j * KSPLIT + c, hh * MLA_V:(hh + 1) * MLA_V, :] for c in range(KSPLIT)]

        return qk, vt, mask, None

    _run_chains([chain(hh) for hh in range(MLA_HPS)], qi * steps_per_q, n_all,
                s_scr, cm_scr, sh_scr, m_scr, acc_scr)
    o_t = jnp.concatenate([acc_scr[hh, :MLA_V] / acc_scr[hh, MLA_V:MLA_V + 1] for hh in range(MLA_HPS)], axis=0)
    o_ref[0] = o_t.T.astype(o_ref.dtype)


def _mla_attention(qm, km, vmt, mask):
    b, s, _ = qm.shape
    t = TQ
    pairs = MLA_HEADS // MLA_HPS
    return pl.pallas_call(
        _mla_kernel,
        grid=(b, pairs, s // t),
        in_specs=[pl.BlockSpec((1, t, MLA_HPS * LANES), lambda bi, hp, qi: (bi, qi, hp)),
                  pl.BlockSpec((1, s, MLA_HPS * LANES), lambda bi, hp, qi: (bi, 0, hp), pipeline_mode=RESIDENT),
                  pl.BlockSpec((1, s // TK, MLA_HPS * MLA_V, TK), lambda bi, hp, qi: (bi, 0, hp, 0),
                               pipeline_mode=RESIDENT),
                  pl.BlockSpec((t, t), lambda bi, hp, qi: (0, 0))],
        out_specs=pl.BlockSpec((1, t, MLA_HPS * MLA_V), lambda bi, hp, qi: (bi, qi, hp)),
        out_shape=jax.ShapeDtypeStruct((b, s, MLA_HEADS * MLA_V), BF16),
        scratch_shapes=_chain_scratch(MLA_HPS, MLA_V),
        compiler_params=_params("parallel", "parallel", "arbitrary"),
        name="mla_attn",
    )(qm, km, vmt, mask)


def _diff_kernel(tab_ref, q_ref, k_ref, vt_ref, bias_ref, lam_ref, g_out_ref, o_ref,
                 s_scr, cm_scr, sh_scr, m_scr, acc_scr):
    hp = pl.program_id(1)
    qi = pl.program_id(2)
    lane = lax.broadcasted_iota(jnp.int32, (1, LANES), 1)
    far_bias = [tab_ref[NUM_BUCKETS // 2 - 1, hp * DIFF_HPS + hh] * LOG2E for hh in range(DIFF_HPS)]
    strip_start = qi * TQ - NEAR_KEYS

    def chain(hh, mp):
        q = q_ref[0, :, hh * LANES:(hh + 1) * LANES]
        in_map = (lane >= DIFF_QK) if mp else (lane < DIFF_QK)
        q = jnp.where(in_map, q, jnp.zeros_like(q))

        def qk(j):
            return [lax.dot_general(k_ref[0, _key_rows(j, c), hh * LANES:(hh + 1) * LANES], q, NT_DIMS,
                                    preferred_element_type=F32) for c in range(KSPLIT)]

        def vt(j):
            return [vt_ref[0, j * KSPLIT + c, hh * DIFF_V:(hh + 1) * DIFF_V, :] for c in range(KSPLIT)]

        def near_bias(j, c):
            off = pl.multiple_of((j * KSPLIT + c) * TK - strip_start, TK)
            return bias_ref[hh, pl.ds(off, TK), :]

        return qk, vt, near_bias, far_bias[hh]

    n_far = jnp.maximum(qi * (TQ // KSTEP) - NEAR_KEYS // KSTEP, 0)
    n_all = (qi + 1) * (TQ // KSTEP)
    _run_chains([chain(hh, mp) for hh in range(DIFF_HPS) for mp in range(2)], n_far, n_all,
                s_scr, cm_scr, sh_scr, m_scr, acc_scr)

    lv = lam_ref[...]
    lam = (jnp.exp(jnp.sum(lv[0:1] * lv[1:2], axis=-1, keepdims=True))
           - jnp.exp(jnp.sum(lv[2:3] * lv[3:4], axis=-1, keepdims=True)) + LAMBDA_INIT)
    for hh in range(DIFF_HPS):
        a0, l0 = acc_scr[2 * hh, :DIFF_V], acc_scr[2 * hh, DIFF_V:DIFF_V + 1]
        a1, l1 = acc_scr[2 * hh + 1, :DIFF_V], acc_scr[2 * hh + 1, DIFF_V:DIFF_V + 1]
        o_t = a0 / l0 - lam * (a1 / l1)
        ms = jnp.sum(o_t * o_t, axis=0, keepdims=True) * (1.0 / DIFF_V)
        o_t = o_t * lax.rsqrt(ms + EPS)
        o_ref[0, :, hh * LANES:(hh + 1) * LANES] = (
            o_t.T * g_out_ref[...] * (1.0 - LAMBDA_INIT)).astype(o_ref.dtype)


def _diff_attention(rel_bias, qd, kd, vdt, bias, lam_vecs, g_out):
    b, s, _ = qd.shape
    t = TQ
    return pl.pallas_call(
        _diff_kernel,
        grid=(b, DIFF_HEADS // DIFF_HPS, s // t),
        in_specs=[pl.BlockSpec(memory_space=pltpu.SMEM),
                  pl.BlockSpec((1, t, DIFF_HPS * LANES), lambda bi, hd, qi: (bi, qi, hd)),
                  pl.BlockSpec((1, s, DIFF_HPS * LANES), lambda bi, hd, qi: (bi, 0, hd), pipeline_mode=RESIDENT),
                  pl.BlockSpec((1, s // TK, DIFF_HPS * DIFF_V, TK), lambda bi, hd, qi: (bi, 0, hd, 0),
                               pipeline_mode=RESIDENT),
                  pl.BlockSpec((DIFF_HPS, NEAR_KEYS + t, t), lambda bi, hd, qi: (hd, 0, 0),
                               pipeline_mode=RESIDENT),
                  pl.BlockSpec(lam_vecs.shape, lambda bi, hd, qi: (0, 0)),
                  pl.BlockSpec(g_out.shape, lambda bi, hd, qi: (0, 0))],
        out_specs=pl.BlockSpec((1, t, DIFF_HPS * LANES), lambda bi, hd, qi: (bi, qi, hd)),
        out_shape=jax.ShapeDtypeStruct((b, s, DIFF_HEADS * DIFF_V), BF16),
        scratch_shapes=_chain_scratch(2 * DIFF_HPS, DIFF_V),
        compiler_params=_params("parallel", "parallel", "arbitrary"),
        name="diff_attn",
    )(rel_bias, qd, kd, vdt, bias, lam_vecs, g_out)


def _ffn_kernel(tiles_per_seq, x_ref, xh_ref, ym_ref, ymh_ref, yd_ref, ydh_ref, p_ref, wom_ref, wod_ref,
                g_ffn_ref, wg_ref, wu_ref, cw_ref, cb_ref, wd_ref, g_ple_ref, wpg_ref, wpp_ref, o_ref, g_scr):
    tm = x_ref.shape[0]
    ext = lambda halo_ref, ref: jnp.concatenate([halo_ref[...], ref[...]], axis=0)
    x1_ext = (ext(xh_ref, x_ref)
              + jnp.dot(ext(ymh_ref, ym_ref), wom_ref[...], preferred_element_type=F32)
              + jnp.dot(ext(ydh_ref, yd_ref), wod_ref[...], preferred_element_type=F32))
    h2_ext = (x1_ext * _rms(x1_ext, x1_ext.shape[-1]) * g_ffn_ref[...]).astype(BF16)
    g_scr[...] = jnp.dot(h2_ext, wg_ref[...], preferred_element_type=F32)

    @pl.when(pl.program_id(0) % tiles_per_seq == 0)
    def _():
        g_scr[0:HALO_ROWS, :] = jnp.zeros((HALO_ROWS, g_scr.shape[1]), F32)

    x1 = x1_ext[HALO_ROWS:]
    h2 = h2_ext[HALO_ROWS:]
    conv = cb_ref[...]
    for j in range(CONV_WIDTH):
        start = HALO_ROWS - (CONV_WIDTH - 1) + j
        conv = conv + g_scr[start:start + tm, :] * cw_ref[j:j + 1, :]
    up = jnp.dot(h2, wu_ref[...], preferred_element_type=F32)
    act = (conv * jax.nn.sigmoid(conv) * up).astype(BF16)
    x2 = x1 + jnp.dot(act, wd_ref[...], preferred_element_type=F32)
    hn = (x2 * _rms(x2, x2.shape[-1]) * g_ple_ref[...]).astype(BF16)
    gate = jax.nn.sigmoid(jnp.dot(hn, wpg_ref[...], preferred_element_type=F32))
    proj = jnp.dot(p_ref[...].astype(BF16), wpp_ref[...], preferred_element_type=F32)
    o_ref[...] = x2 + gate * proj


def _ffn_ple(x2, ym, yd, p2, w_out_m, w_out_d, g_ffn, w_gate, w_up, conv_w, conv_b, w_down, g_ple, w_pg, w_pp,
             seq):
    n, dm = x2.shape
    tm = min(FFN_ROW_TILE, seq)
    d_ff = w_gate.shape[1]
    row = lambda i: (i, 0)
    const = lambda i: (0, 0)
    halo = lambda i: (jnp.maximum(i * (tm // HALO_ROWS) - 1, 0), 0)

    def full(a):
        return pl.BlockSpec(a.shape, const)

    def tile_and_halo(a):
        return [pl.BlockSpec((tm, a.shape[1]), row), pl.BlockSpec((HALO_ROWS, a.shape[1]), halo)]

    return pl.pallas_call(
        functools.partial(_ffn_kernel, seq // tm),
        grid=(n // tm,),
        in_specs=(tile_and_halo(x2) + tile_and_halo(ym) + tile_and_halo(yd)
                  + [pl.BlockSpec((tm, p2.shape[1]), row), full(w_out_m), full(w_out_d), full(g_ffn),
                     full(w_gate), full(w_up), full(conv_w), full(conv_b), full(w_down),
                     full(g_ple), full(w_pg), full(w_pp)]),
        out_specs=pl.BlockSpec((tm, dm), row),
        out_shape=jax.ShapeDtypeStruct((n, dm), F32),
        scratch_shapes=[pltpu.VMEM((tm + HALO_ROWS, d_ff), F32)],
        compiler_params=_params("parallel"),
        name="ffn_ple",
    )(x2, x2, ym, ym, yd, yd, p2, w_out_m, w_out_d, g_ffn, w_gate, w_up, conv_w, conv_b, w_down,
      g_ple, w_pg, w_pp)


def _head_blocks(w, width, n_heads):
    k = w.shape[0]
    w3 = w.reshape(k, n_heads, width)
    return jnp.pad(w3, ((0, 0), (0, 0), (0, LANES - width))).reshape(k, n_heads * LANES)


def _swap_rope_halves(a):
    half = MLA_ROPE // 2
    return jnp.concatenate([a[..., :MLA_NOPE], a[..., MLA_NOPE + half:MLA_QK],
                            a[..., MLA_NOPE:MLA_NOPE + half]], axis=-1)


def _lane_row(g, width=LANES):
    return jnp.pad(g, (0, width - g.shape[0])).reshape(1, width).astype(F32)


def _rope_tables(seq):
    half = MLA_ROPE // 2
    inv_freq = ROPE_THETA ** (-jnp.arange(half, dtype=F32) / half)
    ang = jnp.arange(seq, dtype=jnp.int32).astype(F32)[:, None] * inv_freq[None, :]
    cos, sin = jnp.cos(ang), jnp.sin(ang)
    ones = jnp.ones((seq, MLA_NOPE), F32)
    zeros_n = jnp.zeros((seq, MLA_NOPE), F32)
    zeros_p = jnp.zeros((seq, LANES - MLA_QK), F32)
    cos_t = jnp.concatenate([ones, cos, cos, zeros_p], axis=1)
    sin_t = jnp.concatenate([zeros_n, -sin, sin, zeros_p], axis=1)
    return cos_t, sin_t


def kernel(x, p, attn_norm_g, w_in, q_lat_norm_g, w_uq, kv_lat_norm_g, w_ukv, mla_q_norm_g, mla_k_norm_g,
           diff_q_norm_g, diff_k_norm_g, lambda_q1, lambda_k1, lambda_q2, lambda_k2, diff_out_norm_g,
           rel_bias, w_out, ffn_norm_g, w_gate, w_up, conv_w, conv_b, w_down, ple_norm_g, w_ple_gate,
           w_ple_proj):
    b, s, dm = x.shape
    depth = p.shape[0]
    assert s % TQ == 0 and s % IN_ROW_TILE == 0 and s % FFN_ROW_TILE == 0
    assert TQ % KSTEP == 0 and NEAR_KEYS % KSTEP == 0 and IN_ROW_TILE % TK == 0
    assert depth == 1

    cos_t, sin_t = _rope_tables(s)
    bias, mask = _bias_tiles(rel_bias.astype(F32))
    x2 = x.reshape(b * s, dm)

    for i in range(depth):
        wi = w_in[i]
        off_kr = MLA_Q_RANK + MLA_KV_RANK
        off_dq = off_kr + MLA_ROPE
        off_dv = off_dq + 2 * DIFF_HEADS * DIFF_V
        k_rope = wi[:, off_kr:off_dq]
        half = MLA_ROPE // 2
        k_rope_sw = jnp.concatenate([k_rope[:, half:], k_rope[:, :half]], axis=1)
        lane_pad = ((0, 0), (MLA_NOPE, LANES - MLA_QK))
        w_in_p = jnp.concatenate([wi[:, :off_kr], jnp.pad(k_rope, lane_pad), jnp.pad(k_rope_sw, lane_pad),
                                  wi[:, off_dq:off_dv]], axis=1).astype(BF16)
        w_dvt = wi[:, off_dv:].T.astype(BF16)
        w_uq_p = _head_blocks(w_uq[i], MLA_QK, MLA_HEADS).astype(BF16)
        w_uq_sw = _head_blocks(
            _swap_rope_halves(w_uq[i].reshape(MLA_Q_RANK, MLA_HEADS, MLA_QK)).reshape(MLA_Q_RANK, -1),
            MLA_QK, MLA_HEADS).astype(BF16)
        w_ukv3 = w_ukv[i].reshape(MLA_KV_RANK, MLA_HEADS, MLA_NOPE + MLA_V)
        w_uk_p = _head_blocks(w_ukv3[:, :, :MLA_NOPE].reshape(MLA_KV_RANK, -1), MLA_NOPE, MLA_HEADS).astype(BF16)
        w_uvt = w_ukv3[:, :, MLA_NOPE:].reshape(MLA_KV_RANK, MLA_HEADS * MLA_V).T.astype(BF16)
        gq, gk = mla_q_norm_g[i], mla_k_norm_g[i]
        gdq = jnp.tile(diff_q_norm_g[i], 2).reshape(1, LANES).astype(F32)
        gdk = jnp.tile(diff_k_norm_g[i], 2).reshape(1, LANES).astype(F32)

        qm, km, vmt, qd, kd, vdt = _in_proj(
            x2, cos_t, sin_t, attn_norm_g[i].reshape(1, dm), w_in_p, w_dvt, q_lat_norm_g[i].reshape(1, -1),
            w_uq_p, w_uq_sw, kv_lat_norm_g[i].reshape(1, -1), w_uk_p, w_uvt,
            _lane_row(gq), _lane_row(_swap_rope_halves(gq)), _lane_row(gk), _lane_row(_swap_rope_halves(gk)),
            gdq, gdk, b, s)

        y_mla = _mla_attention(qm.reshape(b, s, -1), km.reshape(b, s, -1), vmt, mask)
        lam_vecs = jnp.stack([lambda_q1[i], lambda_k1[i], lambda_q2[i], lambda_k2[i]]).astype(F32)
        y_diff = _diff_attention(rel_bias.astype(F32), qd.reshape(b, s, -1), kd.reshape(b, s, -1), vdt, bias,
                                 lam_vecs, diff_out_norm_g[i].reshape(1, DIFF_V).astype(F32))

        n_mla = MLA_HEADS * MLA_V
        x2 = _ffn_ple(x2, y_mla.reshape(b * s, -1), y_diff.reshape(b * s, -1), p[i].reshape(b * s, -1),
                      w_out[i][:n_mla].astype(BF16), w_out[i][n_mla:].astype(BF16),
                      ffn_norm_g[i].reshape(1, dm), w_gate[i].astype(BF16), w_up[i].astype(BF16),
                      conv_w[i], conv_b[i].reshape(1, -1), w_down[i].astype(BF16),
                      ple_norm_g[i].reshape(1, dm), w_ple_gate[i].astype(BF16), w_ple_proj[i].astype(BF16), s)
    return x2.reshape(b, s, dm)
```

```python
import functools
import math

import jax
import jax.numpy as jnp
from jax import lax
from jax.experimental import pallas as pl
from jax.experimental.pallas import tpu as pltpu

F32 = jnp.float32
BF16 = jnp.bfloat16

LANES = 128
SUBLANES = 8
VMEM_LIMIT_BYTES = 56 * 1024 * 1024

CHUNK = 64
EPS = 1e-6
NEG_INF = -1e30
MLA_HEADS = 8
MLA_Q_RANK = 256
MLA_KV_RANK = 128
MLA_NOPE = 64
MLA_ROPE = 32
MLA_QK = MLA_NOPE + MLA_ROPE
MLA_V = 64
ROPE_THETA = 10000.0
DIFF_HEADS = 4
DIFF_QK = 64
DIFF_V = 2 * DIFF_QK
NUM_BUCKETS = 32
MAX_DISTANCE = 1024
CONV_WIDTH = 3
LAMBDA_INIT = 0.8 - 0.6 * math.exp(-0.3 * 0)
LOG2E = math.log2(math.e)

TQ = 512
TK = 256
KSPLIT = 2
KSTEP = TK * KSPLIT
SUM_ROWS = 16
NEAR_KEYS = 1024
MLA_HPS = 8
DIFF_HPS = 4
IN_ROW_TILE = 512
FFN_ROW_TILE = 512
HALO_ROWS = 2 * SUBLANES

ZC_QLAT = 0
ZC_KVLAT = ZC_QLAT + MLA_Q_RANK
ZC_KR = ZC_KVLAT + MLA_KV_RANK
ZC_KRSW = ZC_KR + LANES
ZC_DQ = ZC_KRSW + LANES
ZC_DK = ZC_DQ + DIFF_HEADS * DIFF_V
ZC_END = ZC_DK + DIFF_HEADS * DIFF_V

NT_DIMS = (((1,), (1,)), ((), ()))
RESIDENT = pl.Buffered(1)


def _params(*sem):
    return pltpu.CompilerParams(dimension_semantics=sem, vmem_limit_bytes=VMEM_LIMIT_BYTES)


def _rms(x, width):
    return lax.rsqrt(jnp.sum(x * x, axis=-1, keepdims=True) * (1.0 / width) + EPS)


def _bias_kernel(tab_ref, bias_ref, mask_ref):
    t = TQ
    kk = lax.broadcasted_iota(jnp.int32, (t, t), 0)
    qq = lax.broadcasted_iota(jnp.int32, (t, t), 1)
    chunk_bits = CHUNK.bit_length() - 1
    q_chunk = lax.shift_right_logical(qq, chunk_bits)
    mask_ref[...] = jnp.where(lax.shift_right_logical(kk, chunk_bits) <= q_chunk, 0.0, NEG_INF).astype(F32)
    key_off = kk + pl.program_id(0) * t - NEAR_KEYS
    add_mask = jnp.where(lax.shift_right_arithmetic(key_off, chunk_bits) <= q_chunk, 0.0, NEG_INF).astype(F32)
    rel = key_off - qq
    nb = NUM_BUCKETS // 2
    max_exact = nb // 2
    sign_off = (rel > 0).astype(jnp.int32) * nb
    n = jnp.abs(rel)
    nf = jnp.maximum(n, 1).astype(F32)
    large = max_exact + (jnp.log(nf / max_exact) / math.log(MAX_DISTANCE / max_exact)
                         * (nb - max_exact)).astype(jnp.int32)
    large = jnp.minimum(large, nb - 1)
    bucket = sign_off + jnp.where(n < max_exact, n, large)
    for h in range(DIFF_HEADS):
        acc = jnp.zeros((t, t), F32)
        for b in range(NUM_BUCKETS):
            acc = jnp.where(bucket == b, tab_ref[b, h], acc)
        bias_ref[h] = acc * LOG2E + add_mask


def _bias_tiles(rel_bias):
    t = TQ
    n_blocks = (NEAR_KEYS + TQ) // t
    return pl.pallas_call(
        _bias_kernel,
        grid=(n_blocks,),
        in_specs=[pl.BlockSpec(memory_space=pltpu.SMEM)],
        out_specs=[pl.BlockSpec((DIFF_HEADS, t, t), lambda d: (0, d, 0)),
                   pl.BlockSpec((t, t), lambda d: (0, 0))],
        out_shape=[jax.ShapeDtypeStruct((DIFF_HEADS, n_blocks * t, t), F32),
                   jax.ShapeDtypeStruct((t, t), F32)],
        compiler_params=_params("arbitrary"),
        name="bias_tiles",
    )(rel_bias)


def _in_proj_kernel(x_ref, cos_ref, sin_ref, g_attn_ref, w_in_ref, w_dvt_ref, g_ql_ref, w_uq_ref, w_uqsw_ref,
                    g_kvl_ref, w_uk_ref, w_uvt_ref, gq_ref, gqsw_ref, gk_ref, gksw_ref,
                    gdq_ref, gdk_ref,
                    qm_ref, km_ref, vmt_ref, qd_ref, kd_ref, vdt_ref):
    x = x_ref[...]
    h = (x * _rms(x, x.shape[-1]) * g_attn_ref[...]).astype(BF16)
    z = jnp.dot(h, w_in_ref[...], preferred_element_type=F32)

    def store_transposed(dst_ref, w_t, act):
        v_t = lax.dot_general(w_t, act, NT_DIMS, preferred_element_type=F32).astype(BF16)
        for c in range(v_t.shape[1] // TK):
            dst_ref[0, c] = v_t[:, c * TK:(c + 1) * TK]

    store_transposed(vdt_ref, w_dvt_ref[...], h)

    q_lat = z[:, ZC_QLAT:ZC_KVLAT]
    kv_lat = z[:, ZC_KVLAT:ZC_KR]
    kr = z[:, ZC_KR:ZC_KRSW]
    krsw = z[:, ZC_KRSW:ZC_DQ]

    qln = (q_lat * _rms(q_lat, MLA_Q_RANK) * g_ql_ref[...]).astype(BF16)
    q = jnp.dot(qln, w_uq_ref[...], preferred_element_type=F32)
    qsw = jnp.dot(qln, w_uqsw_ref[...], preferred_element_type=F32)
    kvn = (kv_lat * _rms(kv_lat, MLA_KV_RANK) * g_kvl_ref[...]).astype(BF16)
    kn = jnp.dot(kvn, w_uk_ref[...], preferred_element_type=F32)
    store_transposed(vmt_ref, w_uvt_ref[...], kvn)

    cos = cos_ref[...]
    sin = sin_ref[...]
    q_scale = MLA_QK ** -0.5 * LOG2E
    q_cos = cos * (gq_ref[...] * q_scale)
    q_sin = sin * (gqsw_ref[...] * q_scale)
    k_cos = cos * gk_ref[...]
    k_sin = sin * gksw_ref[...]
    for hd in range(MLA_HEADS):
        sl = slice(hd * LANES, (hd + 1) * LANES)
        qh = q[:, sl]
        qm_ref[:, sl] = (_rms(qh, MLA_QK) * (qh * q_cos + qsw[:, sl] * q_sin)).astype(BF16)
        kh = kn[:, sl] + kr
        km_ref[:, sl] = (_rms(kh, MLA_QK) * (kh * k_cos + krsw * k_sin)).astype(BF16)

    lane = lax.broadcasted_iota(jnp.int32, (1, LANES), 1)
    first_map = lane < DIFF_QK
    d_scale = DIFF_QK ** -0.5 * LOG2E
    for hd in range(DIFF_HEADS):
        sl = slice(hd * LANES, (hd + 1) * LANES)
        for src, g_ref, dst, scale in ((ZC_DQ, gdq_ref, qd_ref, d_scale), (ZC_DK, gdk_ref, kd_ref, 1.0)):
            blk = z[:, src + hd * LANES: src + (hd + 1) * LANES]
            sq = blk * blk
            tot = jnp.sum(sq, axis=-1, keepdims=True)
            lo = jnp.sum(jnp.where(first_map, sq, 0.0), axis=-1, keepdims=True)
            ms = jnp.where(first_map, lo, tot - lo) * (1.0 / DIFF_QK)
            dst[:, sl] = (blk * lax.rsqrt(ms + EPS) * (g_ref[...] * scale)).astype(BF16)


def _in_proj(x2, cos_t, sin_t, g_attn, w_in_p, w_dvt, g_ql, w_uq_p, w_uq_sw, g_kvl, w_uk_p, w_uvt,
             gq, gqsw, gk, gksw, gdq, gdk, batch, seq):
    n, dm = x2.shape
    tm = IN_ROW_TILE
    tiles_per_seq = seq // tm
    row = lambda i: (i, 0)
    const = lambda i: (0, 0)
    pos = lambda i: (i % tiles_per_seq, 0)
    tile4 = lambda i: (i // tiles_per_seq, i % tiles_per_seq, 0, 0)
    kt = tm // TK

    def full(a):
        return pl.BlockSpec(a.shape, const)

    def rows_out(width):
        return pl.BlockSpec((tm, width), row), jax.ShapeDtypeStruct((n, width), BF16)

    def transposed_out(width):
        return (pl.BlockSpec((1, kt, width, TK), tile4),
                jax.ShapeDtypeStruct((batch, tiles_per_seq * kt, width, TK), BF16))

    outs = [rows_out(MLA_HEADS * LANES), rows_out(MLA_HEADS * LANES), transposed_out(MLA_HEADS * MLA_V),
            rows_out(DIFF_HEADS * DIFF_V), rows_out(DIFF_HEADS * DIFF_V), transposed_out(DIFF_HEADS * DIFF_V)]
    return pl.pallas_call(
        _in_proj_kernel,
        grid=(n // tm,),
        in_specs=[pl.BlockSpec((tm, dm), row), pl.BlockSpec((tm, LANES), pos), pl.BlockSpec((tm, LANES), pos),
                  full(g_attn), full(w_in_p), full(w_dvt), full(g_ql), full(w_uq_p), full(w_uq_sw),
                  full(g_kvl), full(w_uk_p), full(w_uvt), full(gq), full(gqsw), full(gk), full(gksw),
                  full(gdq), full(gdk)],
        out_specs=[o[0] for o in outs],
        out_shape=[o[1] for o in outs],
        compiler_params=_params("parallel"),
        name="in_proj",
    )(x2, cos_t, sin_t, g_attn, w_in_p, w_dvt, g_ql, w_uq_p, w_uq_sw, g_kvl, w_uk_p, w_uvt,
      gq, gqsw, gk, gksw, gdq, gdk)


def _col_max(s_tiles):
    return functools.reduce(jnp.maximum, [jnp.max(s, axis=0, keepdims=True) for s in s_tiles])


def _key_rows(step_idx, c):
    return pl.ds(pl.multiple_of(step_idx * KSTEP + c * TK, TK), TK)


def _run_chains(chains, n_plain, n_all, first, has_next, next_decorated, s_scr, cm_scr, sh_scr, m_scr, acc_scr):
    row = lax.broadcasted_iota(jnp.int32, (SUM_ROWS, TK), 0)
    ones_rows = jnp.where(row == 0, 1.0, 0.0).astype(BF16)
    use_shift = any(ch["shift"] is not None for ch in chains)

    def park(ci, s_tiles, add_tiles):
        shift = chains[ci]["shift"]
        if add_tiles is not None:
            s_tiles = [s + a for s, a in zip(s_tiles, add_tiles)]
        for c, s in enumerate(s_tiles):
            s_scr[ci, c] = s
        owed = shift if (shift is not None and add_tiles is None) else 0.0
        cm_scr[ci] = functools.reduce(jnp.maximum, [jnp.max(s, axis=0, keepdims=True) for s in s_tiles]) + owed
        if use_shift:
            sh_scr[ci] = jnp.zeros(sh_scr.shape[1:], F32) + owed

    def produce(ci, j, decorated):
        ch = chains[ci]
        park(ci, ch["qk"](j), [ch["add"](j, c) for c in range(KSPLIT)] if decorated else None)

    def produce_next(ci, decorated):
        ch = chains[ci]
        park(ci, ch["qk_next"](), [ch["add_next"](c) for c in range(KSPLIT)] if decorated else None)

    def consume(ci, j):
        vt = chains[ci]["vt"]
        m = m_scr[ci]
        m_new = jnp.maximum(m, cm_scr[ci])
        alpha = jnp.exp2(m - m_new)
        m_sub = m_new - sh_scr[ci] if use_shift else m_new
        acc = alpha * acc_scr[ci]
        for c, v_t in enumerate(vt(j)):
            p = jnp.exp2(s_scr[ci, c] - m_sub).astype(BF16)
            acc = acc + jnp.dot(jnp.concatenate([v_t, ones_rows], axis=0), p, preferred_element_type=F32)
        m_scr[ci] = m_new
        acc_scr[ci] = acc

    n_chains = len(chains)
    for ci in range(n_chains):
        m_scr[ci] = jnp.full(m_scr.shape[1:], NEG_INF, F32)
        acc_scr[ci] = jnp.zeros(acc_scr.shape[1:], F32)

    @pl.when(first)
    def _():
        for ci in range(n_chains):
            produce(ci, 0, True)

    def run(start, stop, decorated):
        def body(j, carry):
            for ci in range(n_chains):
                consume(ci, j)
                produce(ci, j + 1, decorated)
            return carry
        lax.fori_loop(start, stop, body, 0)

    switch = jnp.maximum(n_plain - 1, 0)
    run(0, switch, False)
    run(switch, n_all - 1, True)

    last_variants = [(jnp.logical_not(has_next), None)]
    if next_decorated is False:
        last_variants.append((has_next, False))
    else:
        last_variants += [(jnp.logical_and(has_next, jnp.logical_not(next_decorated)), False),
                          (jnp.logical_and(has_next, next_decorated), True)]
    for pred, kind in last_variants:
        @pl.when(pred)
        def _(kind=kind):
            for ci in range(n_chains):
                consume(ci, n_all - 1)
                if kind is not None:
                    produce_next(ci, kind)


def _chain_scratch(n_chains, dv):
    stat = pltpu.VMEM((n_chains, 1, TQ), F32)
    return [pltpu.VMEM((n_chains, KSPLIT, TK, TQ), F32), stat, stat, stat,
            pltpu.VMEM((n_chains, dv + SUM_ROWS, TQ), F32)]


def _mla_kernel(q_ref, qn_ref, k_ref, vt_ref, mask_ref, o_ref, s_scr, cm_scr, sh_scr, m_scr, acc_scr):
    qi = pl.program_id(2)
    steps_per_q = TQ // KSTEP
    n_all = (qi + 1) * steps_per_q

    def mask(j, c):
        return mask_ref[pl.ds(pl.multiple_of((j * KSPLIT + c) * TK - qi * TQ, TK), TK), :]

    def chain(hh):
        lanes = slice(hh * LANES, (hh + 1) * LANES)

        def scores(q, j):
            return [lax.dot_general(k_ref[0, _key_rows(j, c), lanes], q, NT_DIMS, preferred_element_type=F32)
                    for c in range(KSPLIT)]

        def vt(j):
            return [vt_ref[0, j * KSPLIT + c, hh * MLA_V:(hh + 1) * MLA_V, :] for c in range(KSPLIT)]

        return dict(qk=lambda j: scores(q_ref[0, :, lanes], j), vt=vt, add=mask, shift=None,
                    qk_next=lambda: scores(qn_ref[0, :, lanes], 0), add_next=None)

    _run_chains([chain(hh) for hh in range(MLA_HPS)], qi * steps_per_q, n_all,
                qi == 0, qi < pl.num_programs(2) - 1, False, s_scr, cm_scr, sh_scr, m_scr, acc_scr)
    o_t = jnp.concatenate([acc_scr[hh, :MLA_V] / acc_scr[hh, MLA_V:MLA_V + 1] for hh in range(MLA_HPS)], axis=0)
    o_ref[0] = o_t.T.astype(o_ref.dtype)


def _mla_attention(qm, km, vmt, mask):
    b, s, _ = qm.shape
    t = TQ
    pairs = MLA_HEADS // MLA_HPS
    return pl.pallas_call(
        _mla_kernel,
        grid=(b, pairs, s // t),
        in_specs=[pl.BlockSpec((1, t, MLA_HPS * LANES), lambda bi, hp, qi: (bi, qi, hp)),
                  pl.BlockSpec((1, t, MLA_HPS * LANES),
                               lambda bi, hp, qi: (bi, jnp.minimum(qi + 1, s // t - 1), hp)),
                  pl.BlockSpec((1, s, MLA_HPS * LANES), lambda bi, hp, qi: (bi, 0, hp), pipeline_mode=RESIDENT),
                  pl.BlockSpec((1, s // TK, MLA_HPS * MLA_V, TK), lambda bi, hp, qi: (bi, 0, hp, 0),
                               pipeline_mode=RESIDENT),
                  pl.BlockSpec((t, t), lambda bi, hp, qi: (0, 0))],
        out_specs=pl.BlockSpec((1, t, MLA_HPS * MLA_V), lambda bi, hp, qi: (bi, qi, hp)),
        out_shape=jax.ShapeDtypeStruct((b, s, MLA_HEADS * MLA_V), BF16),
        scratch_shapes=_chain_scratch(MLA_HPS, MLA_V),
        compiler_params=_params("parallel", "parallel", "arbitrary"),
        name="mla_attn",
    )(qm, qm, km, vmt, mask)


def _diff_kernel(tab_ref, q_ref, qn_ref, k_ref, vt_ref, bias_ref, lam_ref, g_out_ref, o_ref,
                 s_scr, cm_scr, sh_scr, m_scr, acc_scr):
    hp = pl.program_id(1)
    qi = pl.program_id(2)
    lane = lax.broadcasted_iota(jnp.int32, (1, LANES), 1)
    far_bias = [tab_ref[NUM_BUCKETS // 2 - 1, hp * DIFF_HPS + hh] * LOG2E for hh in range(DIFF_HPS)]
    steps_per_q = TQ // KSTEP

    def n_far_of(tile):
        return jnp.maximum(tile * steps_per_q - NEAR_KEYS // KSTEP, 0)

    def chain(hh, mp):
        lanes = slice(hh * LANES, (hh + 1) * LANES)
        in_map = (lane >= DIFF_QK) if mp else (lane < DIFF_QK)

        def scores(q_tile_ref, j):
            q = q_tile_ref[0, :, lanes]
            q = jnp.where(in_map, q, jnp.zeros_like(q))
            return [lax.dot_general(k_ref[0, _key_rows(j, c), lanes], q, NT_DIMS, preferred_element_type=F32)
                    for c in range(KSPLIT)]

        def vt(j):
            return [vt_ref[0, j * KSPLIT + c, hh * DIFF_V:(hh + 1) * DIFF_V, :] for c in range(KSPLIT)]

        def bias_rows(tile, j, c):
            off = pl.multiple_of((j * KSPLIT + c) * TK - (tile * TQ - NEAR_KEYS), TK)
            return bias_ref[hh, pl.ds(off, TK), :]

        return dict(qk=lambda j: scores(q_ref, j), vt=vt, add=lambda j, c: bias_rows(qi, j, c),
                    shift=far_bias[hh], qk_next=lambda: scores(qn_ref, 0),
                    add_next=lambda c: bias_rows(qi + 1, 0, c))

    _run_chains([chain(hh, mp) for hh in range(DIFF_HPS) for mp in range(2)],
                n_far_of(qi), (qi + 1) * steps_per_q, qi == 0, qi < pl.num_programs(2) - 1,
                n_far_of(qi + 1) == 0, s_scr, cm_scr, sh_scr, m_scr, acc_scr)

    lv = lam_ref[...]
    lam = (jnp.exp(jnp.sum(lv[0:1] * lv[1:2], axis=-1, keepdims=True))
           - jnp.exp(jnp.sum(lv[2:3] * lv[3:4], axis=-1, keepdims=True)) + LAMBDA_INIT)
    for hh in range(DIFF_HPS):
        a0, l0 = acc_scr[2 * hh, :DIFF_V], acc_scr[2 * hh, DIFF_V:DIFF_V + 1]
        a1, l1 = acc_scr[2 * hh + 1, :DIFF_V], acc_scr[2 * hh + 1, DIFF_V:DIFF_V + 1]
        o_t = a0 / l0 - lam * (a1 / l1)
        ms = jnp.sum(o_t * o_t, axis=0, keepdims=True) * (1.0 / DIFF_V)
        o_t = o_t * lax.rsqrt(ms + EPS)
        o_ref[0, :, hh * LANES:(hh + 1) * LANES] = (
            o_t.T * g_out_ref[...] * (1.0 - LAMBDA_INIT)).astype(o_ref.dtype)


def _diff_attention(rel_bias, qd, kd, vdt, bias, lam_vecs, g_out):
    b, s, _ = qd.shape
    t = TQ
    return pl.pallas_call(
        _diff_kernel,
        grid=(b, DIFF_HEADS // DIFF_HPS, s // t),
        in_specs=[pl.BlockSpec(memory_space=pltpu.SMEM),
                  pl.BlockSpec((1, t, DIFF_HPS * LANES), lambda bi, hd, qi: (bi, qi, hd)),
                  pl.BlockSpec((1, t, DIFF_HPS * LANES),
                               lambda bi, hd, qi: (bi, jnp.minimum(qi + 1, s // t - 1), hd)),
                  pl.BlockSpec((1, s, DIFF_HPS * LANES), lambda bi, hd, qi: (bi, 0, hd), pipeline_mode=RESIDENT),
                  pl.BlockSpec((1, s // TK, DIFF_HPS * DIFF_V, TK), lambda bi, hd, qi: (bi, 0, hd, 0),
                               pipeline_mode=RESIDENT),
                  pl.BlockSpec((DIFF_HPS, NEAR_KEYS + t, t), lambda bi, hd, qi: (hd, 0, 0),
                               pipeline_mode=RESIDENT),
                  pl.BlockSpec(lam_vecs.shape, lambda bi, hd, qi: (0, 0)),
                  pl.BlockSpec(g_out.shape, lambda bi, hd, qi: (0, 0))],
        out_specs=pl.BlockSpec((1, t, DIFF_HPS * LANES), lambda bi, hd, qi: (bi, qi, hd)),
        out_shape=jax.ShapeDtypeStruct((b, s, DIFF_HEADS * DIFF_V), BF16),
        scratch_shapes=_chain_scratch(2 * DIFF_HPS, DIFF_V),
        compiler_params=_params("parallel", "parallel", "arbitrary"),
        name="diff_attn",
    )(rel_bias, qd, qd, kd, vdt, bias, lam_vecs, g_out)


def _ffn_kernel(tiles_per_seq, x_ref, xh_ref, ym_ref, ymh_ref, yd_ref, ydh_ref, p_ref, wom_ref, wod_ref,
                g_ffn_ref, wg_ref, wu_ref, cw_ref, cb_ref, wd_ref, g_ple_ref, wpg_ref, wpp_ref, o_ref, g_scr):
    tm = x_ref.shape[0]
    ext = lambda halo_ref, ref: jnp.concatenate([halo_ref[...], ref[...]], axis=0)
    x1_ext = (ext(xh_ref, x_ref)
              + jnp.dot(ext(ymh_ref, ym_ref), wom_ref[...], preferred_element_type=F32)
              + jnp.dot(ext(ydh_ref, yd_ref), wod_ref[...], preferred_element_type=F32))
    h2_ext = (x1_ext * _rms(x1_ext, x1_ext.shape[-1]) * g_ffn_ref[...]).astype(BF16)
    g_scr[...] = jnp.dot(h2_ext, wg_ref[...], preferred_element_type=F32)

    @pl.when(pl.program_id(0) % tiles_per_seq == 0)
    def _():
        g_scr[0:HALO_ROWS, :] = jnp.zeros((HALO_ROWS, g_scr.shape[1]), F32)

    x1 = x1_ext[HALO_ROWS:]
    h2 = h2_ext[HALO_ROWS:]
    conv = cb_ref[...]
    for j in range(CONV_WIDTH):
        start = HALO_ROWS - (CONV_WIDTH - 1) + j
        conv = conv + g_scr[start:start + tm, :] * cw_ref[j:j + 1, :]
    up = jnp.dot(h2, wu_ref[...], preferred_element_type=F32)
    act = (conv * jax.nn.sigmoid(conv) * up).astype(BF16)
    x2 = x1 + jnp.dot(act, wd_ref[...], preferred_element_type=F32)
    hn = (x2 * _rms(x2, x2.shape[-1]) * g_ple_ref[...]).astype(BF16)
    gate = jax.nn.sigmoid(jnp.dot(hn, wpg_ref[...], preferred_element_type=F32))
    proj = jnp.dot(p_ref[...].astype(BF16), wpp_ref[...], preferred_element_type=F32)
    o_ref[...] = x2 + gate * proj


def _ffn_ple(x2, ym, yd, p2, w_out_m, w_out_d, g_ffn, w_gate, w_up, conv_w, conv_b, w_down, g_ple, w_pg, w_pp,
             seq):
    n, dm = x2.shape
    tm = min(FFN_ROW_TILE, seq)
    d_ff = w_gate.shape[1]
    row = lambda i: (i, 0)
    const = lambda i: (0, 0)
    halo = lambda i: (jnp.maximum(i * (tm // HALO_ROWS) - 1, 0), 0)

    def full(a):
        return pl.BlockSpec(a.shape, const)

    def tile_and_halo(a):
        return [pl.BlockSpec((tm, a.shape[1]), row), pl.BlockSpec((HALO_ROWS, a.shape[1]), halo)]

    return pl.pallas_call(
        functools.partial(_ffn_kernel, seq // tm),
        grid=(n // tm,),
        in_specs=(tile_and_halo(x2) + tile_and_halo(ym) + tile_and_halo(yd)
                  + [pl.BlockSpec((tm, p2.shape[1]), row), full(w_out_m), full(w_out_d), full(g_ffn),
                     full(w_gate), full(w_up), full(conv_w), full(conv_b), full(w_down),
                     full(g_ple), full(w_pg), full(w_pp)]),
        out_specs=pl.BlockSpec((tm, dm), row),
        out_shape=jax.ShapeDtypeStruct((n, dm), F32),
        scratch_shapes=[pltpu.VMEM((tm + HALO_ROWS, d_ff), F32)],
        compiler_params=_params("parallel"),
        name="ffn_ple",
    )(x2, x2, ym, ym, yd, yd, p2, w_out_m, w_out_d, g_ffn, w_gate, w_up, conv_w, conv_b, w_down,
      g_ple, w_pg, w_pp)


def _head_blocks(w, width, n_heads):
    k = w.shape[0]
    w3 = w.reshape(k, n_heads, width)
    return jnp.pad(w3, ((0, 0), (0, 0), (0, LANES - width))).reshape(k, n_heads * LANES)


def _swap_rope_halves(a):
    half = MLA_ROPE // 2
    return jnp.concatenate([a[..., :MLA_NOPE], a[..., MLA_NOPE + half:MLA_QK],
                            a[..., MLA_NOPE:MLA_NOPE + half]], axis=-1)


def _lane_row(g, width=LANES):
    return jnp.pad(g, (0, width - g.shape[0])).reshape(1, width).astype(F32)


def _rope_tables(seq):
    half = MLA_ROPE // 2
    inv_freq = ROPE_THETA ** (-jnp.arange(half, dtype=F32) / half)
    ang = jnp.arange(seq, dtype=jnp.int32).astype(F32)[:, None] * inv_freq[None, :]
    cos, sin = jnp.cos(ang), jnp.sin(ang)
    ones = jnp.ones((seq, MLA_NOPE), F32)
    zeros_n = jnp.zeros((seq, MLA_NOPE), F32)
    zeros_p = jnp.zeros((seq, LANES - MLA_QK), F32)
    cos_t = jnp.concatenate([ones, cos, cos, zeros_p], axis=1)
    sin_t = jnp.concatenate([zeros_n, -sin, sin, zeros_p], axis=1)
    return cos_t, sin_t


def kernel(x, p, attn_norm_g, w_in, q_lat_norm_g, w_uq, kv_lat_norm_g, w_ukv, mla_q_norm_g, mla_k_norm_g,
           diff_q_norm_g, diff_k_norm_g, lambda_q1, lambda_k1, lambda_q2, lambda_k2, diff_out_norm_g,
           rel_bias, w_out, ffn_norm_g, w_gate, w_up, conv_w, conv_b, w_down, ple_norm_g, w_ple_gate,
           w_ple_proj):
    b, s, dm = x.shape
    depth = p.shape[0]
    assert s % TQ == 0 and s % IN_ROW_TILE == 0 and s % FFN_ROW_TILE == 0
    assert TQ % KSTEP == 0 and NEAR_KEYS % KSTEP == 0 and IN_ROW_TILE % TK == 0
    assert depth == 1

    cos_t, sin_t = _rope_tables(s)
    bias, mask = _bias_tiles(rel_bias.astype(F32))
    x2 = x.reshape(b * s, dm)

    for i in range(depth):
        wi = w_in[i]
        off_kr = MLA_Q_RANK + MLA_KV_RANK
        off_dq = off_kr + MLA_ROPE
        off_dv = off_dq + 2 * DIFF_HEADS * DIFF_V
        k_rope = wi[:, off_kr:off_dq]
        half = MLA_ROPE // 2
        k_rope_sw = jnp.concatenate([k_rope[:, half:], k_rope[:, :half]], axis=1)
        lane_pad = ((0, 0), (MLA_NOPE, LANES - MLA_QK))
        w_in_p = jnp.concatenate([wi[:, :off_kr], jnp.pad(k_rope, lane_pad), jnp.pad(k_rope_sw, lane_pad),
                                  wi[:, off_dq:off_dv]], axis=1).astype(BF16)
        w_dvt = wi[:, off_dv:].T.astype(BF16)
        w_uq_p = _head_blocks(w_uq[i], MLA_QK, MLA_HEADS).astype(BF16)
        w_uq_sw = _head_blocks(
            _swap_rope_halves(w_uq[i].reshape(MLA_Q_RANK, MLA_HEADS, MLA_QK)).reshape(MLA_Q_RANK, -1),
            MLA_QK, MLA_HEADS).astype(BF16)
        w_ukv3 = w_ukv[i].reshape(MLA_KV_RANK, MLA_HEADS, MLA_NOPE + MLA_V)
        w_uk_p = _head_blocks(w_ukv3[:, :, :MLA_NOPE].reshape(MLA_KV_RANK, -1), MLA_NOPE, MLA_HEADS).astype(BF16)
        w_uvt = w_ukv3[:, :, MLA_NOPE:].reshape(MLA_KV_RANK, MLA_HEADS * MLA_V).T.astype(BF16)
        gq, gk = mla_q_norm_g[i], mla_k_norm_g[i]
        gdq = jnp.tile(diff_q_norm_g[i], 2).reshape(1, LANES).astype(F32)
        gdk = jnp.tile(diff_k_norm_g[i], 2).reshape(1, LANES).astype(F32)

        qm, km, vmt, qd, kd, vdt = _in_proj(
            x2, cos_t, sin_t, attn_norm_g[i].reshape(1, dm), w_in_p, w_dvt, q_lat_norm_g[i].reshape(1, -1),
            w_uq_p, w_uq_sw, kv_lat_norm_g[i].reshape(1, -1), w_uk_p, w_uvt,
            _lane_row(gq), _lane_row(_swap_rope_halves(gq)), _lane_row(gk), _lane_row(_swap_rope_halves(gk)),
            gdq, gdk, b, s)

        y_mla = _mla_attention(qm.reshape(b, s, -1), km.reshape(b, s, -1), vmt, mask)
        lam_vecs = jnp.stack([lambda_q1[i], lambda_k1[i], lambda_q2[i], lambda_k2[i]]).astype(F32)
        y_diff = _diff_attention(rel_bias.astype(F32), qd.reshape(b, s, -1), kd.reshape(b, s, -1), vdt, bias,
                                 lam_vecs, diff_out_norm_g[i].reshape(1, DIFF_V).astype(F32))

        n_mla = MLA_HEADS * MLA_V
        x2 = _ffn_ple(x2, y_mla.reshape(b * s, -1), y_diff.reshape(b * s, -1), p[i].reshape(b * s, -1),
                      w_out[i][:n_mla].astype(BF16), w_out[i][n_mla:].astype(BF16),
                      ffn_norm_g[i].reshape(1, dm), w_gate[i].astype(BF16), w_up[i].astype(BF16),
                      conv_w[i], conv_b[i].reshape(1, -1), w_down[i].astype(BF16),
                      ple_norm_g[i].reshape(1, dm), w_ple_gate[i].astype(BF16), w_ple_proj[i].astype(BF16), s)
    return x2.reshape(b, s, dm)
```

```python
import functools
import math

import jax
import jax.numpy as jnp
from jax import lax
from jax.experimental import pallas as pl
from jax.experimental.pallas import tpu as pltpu

F32 = jnp.float32
BF16 = jnp.bfloat16

LANES = 128
SUBLANES = 8
VMEM_LIMIT_BYTES = 56 * 1024 * 1024

CHUNK = 64
EPS = 1e-6
NEG_INF = -1e30
MLA_HEADS = 8
MLA_Q_RANK = 256
MLA_KV_RANK = 128
MLA_NOPE = 64
MLA_ROPE = 32
MLA_QK = MLA_NOPE + MLA_ROPE
MLA_V = 64
ROPE_THETA = 10000.0
DIFF_HEADS = 4
DIFF_QK = 64
DIFF_V = 2 * DIFF_QK
NUM_BUCKETS = 32
MAX_DISTANCE = 1024
CONV_WIDTH = 3
LAMBDA_INIT = 0.8 - 0.6 * math.exp(-0.3 * 0)
LOG2E = math.log2(math.e)

TQ = 512
TK = 256
KSPLIT = 2
KSTEP = TK * KSPLIT
SUM_ROWS = 16
NEAR_KEYS = 1024
MLA_HPS = 8
DIFF_HPS = 4
IN_ROW_TILE = 512
FFN_ROW_TILE = 512
HALO_ROWS = 2 * SUBLANES

ZC_QLAT = 0
ZC_KVLAT = ZC_QLAT + MLA_Q_RANK
ZC_KR = ZC_KVLAT + MLA_KV_RANK
ZC_KRSW = ZC_KR + LANES
ZC_DQ = ZC_KRSW + LANES
ZC_DK = ZC_DQ + DIFF_HEADS * DIFF_V
ZC_END = ZC_DK + DIFF_HEADS * DIFF_V

NT_DIMS = (((1,), (1,)), ((), ()))
RESIDENT = pl.Buffered(1)


def _params(*sem):
    return pltpu.CompilerParams(dimension_semantics=sem, vmem_limit_bytes=VMEM_LIMIT_BYTES)


def _rms(x, width):
    return lax.rsqrt(jnp.sum(x * x, axis=-1, keepdims=True) * (1.0 / width) + EPS)


def _bias_kernel(tab_ref, bias_ref, mask_ref):
    t = TQ
    kk = lax.broadcasted_iota(jnp.int32, (t, t), 0)
    qq = lax.broadcasted_iota(jnp.int32, (t, t), 1)
    chunk_bits = CHUNK.bit_length() - 1
    q_chunk = lax.shift_right_logical(qq, chunk_bits)
    mask_ref[...] = jnp.where(lax.shift_right_logical(kk, chunk_bits) <= q_chunk, 0.0, NEG_INF).astype(F32)
    key_off = kk + pl.program_id(0) * t - NEAR_KEYS
    add_mask = jnp.where(lax.shift_right_arithmetic(key_off, chunk_bits) <= q_chunk, 0.0, NEG_INF).astype(F32)
    rel = key_off - qq
    nb = NUM_BUCKETS // 2
    max_exact = nb // 2
    sign_off = (rel > 0).astype(jnp.int32) * nb
    n = jnp.abs(rel)
    nf = jnp.maximum(n, 1).astype(F32)
    large = max_exact + (jnp.log(nf / max_exact) / math.log(MAX_DISTANCE / max_exact)
                         * (nb - max_exact)).astype(jnp.int32)
    large = jnp.minimum(large, nb - 1)
    bucket = sign_off + jnp.where(n < max_exact, n, large)
    for h in range(DIFF_HEADS):
        acc = jnp.zeros((t, t), F32)
        for b in range(NUM_BUCKETS):
            acc = jnp.where(bucket == b, tab_ref[b, h], acc)
        bias_ref[h] = acc * LOG2E + add_mask


def _bias_tiles(rel_bias):
    t = TQ
    n_blocks = (NEAR_KEYS + TQ) // t
    return pl.pallas_call(
        _bias_kernel,
        grid=(n_blocks,),
        in_specs=[pl.BlockSpec(memory_space=pltpu.SMEM)],
        out_specs=[pl.BlockSpec((DIFF_HEADS, t, t), lambda d: (0, d, 0)),
                   pl.BlockSpec((t, t), lambda d: (0, 0))],
        out_shape=[jax.ShapeDtypeStruct((DIFF_HEADS, n_blocks * t, t), F32),
                   jax.ShapeDtypeStruct((t, t), F32)],
        compiler_params=_params("arbitrary"),
        name="bias_tiles",
    )(rel_bias)


def _in_proj_kernel(x_ref, cos_ref, sin_ref, g_attn_ref, w_in_ref, w_dvt_ref, g_ql_ref, w_uq_ref, w_uqsw_ref,
                    g_kvl_ref, w_uk_ref, w_uvt_ref, gq_ref, gqsw_ref, gk_ref, gksw_ref,
                    gdq_ref, gdk_ref,
                    qm_ref, km_ref, vmt_ref, qd_ref, kd_ref, vdt_ref):
    x = x_ref[...]
    h = (x * _rms(x, x.shape[-1]) * g_attn_ref[...]).astype(BF16)
    z = jnp.dot(h, w_in_ref[...], preferred_element_type=F32)

    def store_transposed(dst_ref, w_t, act):
        v_t = lax.dot_general(w_t, act, NT_DIMS, preferred_element_type=F32).astype(BF16)
        for c in range(v_t.shape[1] // TK):
            dst_ref[0, c] = v_t[:, c * TK:(c + 1) * TK]

    store_transposed(vdt_ref, w_dvt_ref[...], h)

    q_lat = z[:, ZC_QLAT:ZC_KVLAT]
    kv_lat = z[:, ZC_KVLAT:ZC_KR]
    kr = z[:, ZC_KR:ZC_KRSW]
    krsw = z[:, ZC_KRSW:ZC_DQ]

    qln = (q_lat * _rms(q_lat, MLA_Q_RANK) * g_ql_ref[...]).astype(BF16)
    q = jnp.dot(qln, w_uq_ref[...], preferred_element_type=F32)
    qsw = jnp.dot(qln, w_uqsw_ref[...], preferred_element_type=F32)
    kvn = (kv_lat * _rms(kv_lat, MLA_KV_RANK) * g_kvl_ref[...]).astype(BF16)
    kn = jnp.dot(kvn, w_uk_ref[...], preferred_element_type=F32)
    store_transposed(vmt_ref, w_uvt_ref[...], kvn)

    cos = cos_ref[...]
    sin = sin_ref[...]
    q_scale = MLA_QK ** -0.5 * LOG2E
    q_cos = cos * (gq_ref[...] * q_scale)
    q_sin = sin * (gqsw_ref[...] * q_scale)
    k_cos = cos * gk_ref[...]
    k_sin = sin * gksw_ref[...]
    for hd in range(MLA_HEADS):
        sl = slice(hd * LANES, (hd + 1) * LANES)
        qh = q[:, sl]
        qm_ref[hd] = (_rms(qh, MLA_QK) * (qh * q_cos + qsw[:, sl] * q_sin)).astype(BF16)
        kh = kn[:, sl] + kr
        km_ref[hd] = (_rms(kh, MLA_QK) * (kh * k_cos + krsw * k_sin)).astype(BF16)

    lane = lax.broadcasted_iota(jnp.int32, (1, LANES), 1)
    first_map = lane < DIFF_QK
    d_scale = DIFF_QK ** -0.5 * LOG2E
    for hd in range(DIFF_HEADS):
        sl = slice(hd * LANES, (hd + 1) * LANES)
        for src, g_ref, dst, scale in ((ZC_DQ, gdq_ref, qd_ref, d_scale), (ZC_DK, gdk_ref, kd_ref, 1.0)):
            blk = z[:, src + hd * LANES: src + (hd + 1) * LANES]
            sq = blk * blk
            tot = jnp.sum(sq, axis=-1, keepdims=True)
            lo = jnp.sum(jnp.where(first_map, sq, 0.0), axis=-1, keepdims=True)
            ms = jnp.where(first_map, lo, tot - lo) * (1.0 / DIFF_QK)
            dst[hd] = (blk * lax.rsqrt(ms + EPS) * (g_ref[...] * scale)).astype(BF16)


def _in_proj(x2, cos_t, sin_t, g_attn, w_in_p, w_dvt, g_ql, w_uq_p, w_uq_sw, g_kvl, w_uk_p, w_uvt,
             gq, gqsw, gk, gksw, gdq, gdk, batch, seq):
    n, dm = x2.shape
    tm = IN_ROW_TILE
    tiles_per_seq = seq // tm
    row = lambda i: (i, 0)
    const = lambda i: (0, 0)
    pos = lambda i: (i % tiles_per_seq, 0)
    tile4 = lambda i: (i // tiles_per_seq, i % tiles_per_seq, 0, 0)
    kt = tm // TK

    def full(a):
        return pl.BlockSpec(a.shape, const)

    def heads_out(heads):
        return (pl.BlockSpec((heads, tm, LANES), lambda i: (0, i, 0)),
                jax.ShapeDtypeStruct((heads, n, LANES), BF16))

    def transposed_out(width):
        return (pl.BlockSpec((1, kt, width, TK), tile4),
                jax.ShapeDtypeStruct((batch, tiles_per_seq * kt, width, TK), BF16))

    outs = [heads_out(MLA_HEADS), heads_out(MLA_HEADS), transposed_out(MLA_HEADS * MLA_V),
            heads_out(DIFF_HEADS), heads_out(DIFF_HEADS), transposed_out(DIFF_HEADS * DIFF_V)]
    return pl.pallas_call(
        _in_proj_kernel,
        grid=(n // tm,),
        in_specs=[pl.BlockSpec((tm, dm), row), pl.BlockSpec((tm, LANES), pos), pl.BlockSpec((tm, LANES), pos),
                  full(g_attn), full(w_in_p), full(w_dvt), full(g_ql), full(w_uq_p), full(w_uq_sw),
                  full(g_kvl), full(w_uk_p), full(w_uvt), full(gq), full(gqsw), full(gk), full(gksw),
                  full(gdq), full(gdk)],
        out_specs=[o[0] for o in outs],
        out_shape=[o[1] for o in outs],
        compiler_params=_params("parallel"),
        name="in_proj",
    )(x2, cos_t, sin_t, g_attn, w_in_p, w_dvt, g_ql, w_uq_p, w_uq_sw, g_kvl, w_uk_p, w_uvt,
      gq, gqsw, gk, gksw, gdq, gdk)


def _col_max(s_tiles):
    return functools.reduce(jnp.maximum, [jnp.max(s, axis=0, keepdims=True) for s in s_tiles])


def _key_rows(step_idx, c):
    return pl.ds(pl.multiple_of(step_idx * KSTEP + c * TK, TK), TK)


def _run_chains(chains, n_plain, n_all, first, has_next, next_decorated, s_scr, cm_scr, sh_scr, m_scr, acc_scr):
    row = lax.broadcasted_iota(jnp.int32, (SUM_ROWS, TK), 0)
    ones_rows = jnp.where(row == 0, 1.0, 0.0).astype(BF16)
    use_shift = any(ch["shift"] is not None for ch in chains)

    def park(ci, s_tiles, add_tiles):
        shift = chains[ci]["shift"]
        if add_tiles is not None:
            s_tiles = [s + a for s, a in zip(s_tiles, add_tiles)]
        for c, s in enumerate(s_tiles):
            s_scr[ci, c] = s
        owed = shift if (shift is not None and add_tiles is None) else 0.0
        cm_scr[ci] = functools.reduce(jnp.maximum, [jnp.max(s, axis=0, keepdims=True) for s in s_tiles]) + owed
        if use_shift:
            sh_scr[ci] = jnp.zeros(sh_scr.shape[1:], F32) + owed

    def produce(ci, j, decorated):
        ch = chains[ci]
        park(ci, ch["qk"](j), [ch["add"](j, c) for c in range(KSPLIT)] if decorated else None)

    def produce_next(ci, decorated):
        ch = chains[ci]
        park(ci, ch["qk_next"](), [ch["add_next"](c) for c in range(KSPLIT)] if decorated else None)

    def consume(ci, j):
        vt = chains[ci]["vt"]
        m = m_scr[ci]
        m_new = jnp.maximum(m, cm_scr[ci])
        alpha = jnp.exp2(m - m_new)
        m_sub = m_new - sh_scr[ci] if use_shift else m_new
        acc = alpha * acc_scr[ci]
        for c, v_t in enumerate(vt(j)):
            p = jnp.exp2(s_scr[ci, c] - m_sub).astype(BF16)
            acc = acc + jnp.dot(jnp.concatenate([v_t, ones_rows], axis=0), p, preferred_element_type=F32)
        m_scr[ci] = m_new
        acc_scr[ci] = acc

    n_chains = len(chains)
    for ci in range(n_chains):
        m_scr[ci] = jnp.full(m_scr.shape[1:], NEG_INF, F32)
        acc_scr[ci] = jnp.zeros(acc_scr.shape[1:], F32)

    @pl.when(first)
    def _():
        for ci in range(n_chains):
            produce(ci, 0, True)

    def run(start, stop, decorated):
        def body(j, carry):
            for ci in range(n_chains):
                consume(ci, j)
                produce(ci, j + 1, decorated)
            return carry
        lax.fori_loop(start, stop, body, 0)

    switch = jnp.maximum(n_plain - 1, 0)
    run(0, switch, False)
    run(switch, n_all - 1, True)

    last_variants = [(jnp.logical_not(has_next), None)]
    if next_decorated is False:
        last_variants.append((has_next, False))
    else:
        last_variants += [(jnp.logical_and(has_next, jnp.logical_not(next_decorated)), False),
                          (jnp.logical_and(has_next, next_decorated), True)]
    for pred, kind in last_variants:
        @pl.when(pred)
        def _(kind=kind):
            for ci in range(n_chains):
                consume(ci, n_all - 1)
                if kind is not None:
                    produce_next(ci, kind)


def _chain_scratch(n_chains, dv):
    stat = pltpu.VMEM((n_chains, 1, TQ), F32)
    return [pltpu.VMEM((n_chains, KSPLIT, TK, TQ), F32), stat, stat, stat,
            pltpu.VMEM((n_chains, dv + SUM_ROWS, TQ), F32)]


def _mla_kernel(q_ref, qn_ref, k_ref, vt_ref, mask_ref, o_ref, s_scr, cm_scr, sh_scr, m_scr, acc_scr):
    qi = pl.program_id(2)
    steps_per_q = TQ // KSTEP
    n_all = (qi + 1) * steps_per_q

    def mask(j, c):
        return mask_ref[pl.ds(pl.multiple_of((j * KSPLIT + c) * TK - qi * TQ, TK), TK), :]

    def chain(hh):
        def scores(q, j):
            return [lax.dot_general(k_ref[hh, _key_rows(j, c), :], q, NT_DIMS, preferred_element_type=F32)
                    for c in range(KSPLIT)]

        def vt(j):
            return [vt_ref[0, j * KSPLIT + c, hh * MLA_V:(hh + 1) * MLA_V, :] for c in range(KSPLIT)]

        return dict(qk=lambda j: scores(q_ref[hh], j), vt=vt, add=mask, shift=None,
                    qk_next=lambda: scores(qn_ref[hh], 0), add_next=None)

    _run_chains([chain(hh) for hh in range(MLA_HPS)], qi * steps_per_q, n_all,
                qi == 0, qi < pl.num_programs(2) - 1, False, s_scr, cm_scr, sh_scr, m_scr, acc_scr)
    o_t = jnp.concatenate([acc_scr[hh, :MLA_V] / acc_scr[hh, MLA_V:MLA_V + 1] for hh in range(MLA_HPS)], axis=0)
    o_ref[0] = o_t.T.astype(o_ref.dtype)


def _mla_attention(qm, km, vmt, mask, b, s):
    t = TQ
    nq = s // t
    pairs = MLA_HEADS // MLA_HPS
    return pl.pallas_call(
        _mla_kernel,
        grid=(b, pairs, nq),
        in_specs=[pl.BlockSpec((MLA_HPS, t, LANES), lambda bi, hp, qi: (hp, bi * nq + qi, 0)),
                  pl.BlockSpec((MLA_HPS, t, LANES),
                               lambda bi, hp, qi: (hp, bi * nq + jnp.minimum(qi + 1, nq - 1), 0)),
                  pl.BlockSpec((MLA_HPS, s, LANES), lambda bi, hp, qi: (hp, bi, 0), pipeline_mode=RESIDENT),
                  pl.BlockSpec((1, s // TK, MLA_HPS * MLA_V, TK), lambda bi, hp, qi: (bi, 0, hp, 0),
                               pipeline_mode=RESIDENT),
                  pl.BlockSpec((t, t), lambda bi, hp, qi: (0, 0))],
        out_specs=pl.BlockSpec((1, t, MLA_HPS * MLA_V), lambda bi, hp, qi: (bi, qi, hp)),
        out_shape=jax.ShapeDtypeStruct((b, s, MLA_HEADS * MLA_V), BF16),
        scratch_shapes=_chain_scratch(MLA_HPS, MLA_V),
        compiler_params=_params("parallel", "parallel", "arbitrary"),
        name="mla_attn",
    )(qm, qm, km, vmt, mask)


def _diff_kernel(tab_ref, q_ref, qn_ref, k_ref, vt_ref, bias_ref, lam_ref, g_out_ref, o_ref,
                 s_scr, cm_scr, sh_scr, m_scr, acc_scr):
    hp = pl.program_id(1)
    qi = pl.program_id(2)
    lane = lax.broadcasted_iota(jnp.int32, (1, LANES), 1)
    far_bias = [tab_ref[NUM_BUCKETS // 2 - 1, hp * DIFF_HPS + hh] * LOG2E for hh in range(DIFF_HPS)]
    steps_per_q = TQ // KSTEP

    def n_far_of(tile):
        return jnp.maximum(tile * steps_per_q - NEAR_KEYS // KSTEP, 0)

    def chain(hh, mp):
        in_map = (lane >= DIFF_QK) if mp else (lane < DIFF_QK)

        def scores(q_tile_ref, j):
            q = q_tile_ref[hh]
            q = jnp.where(in_map, q, jnp.zeros_like(q))
            return [lax.dot_general(k_ref[hh, _key_rows(j, c), :], q, NT_DIMS, preferred_element_type=F32)
                    for c in range(KSPLIT)]

        def vt(j):
            return [vt_ref[0, j * KSPLIT + c, hh * DIFF_V:(hh + 1) * DIFF_V, :] for c in range(KSPLIT)]

        def bias_rows(tile, j, c):
            off = pl.multiple_of((j * KSPLIT + c) * TK - (tile * TQ - NEAR_KEYS), TK)
            return bias_ref[hh, pl.ds(off, TK), :]

        return dict(qk=lambda j: scores(q_ref, j), vt=vt, add=lambda j, c: bias_rows(qi, j, c),
                    shift=far_bias[hh], qk_next=lambda: scores(qn_ref, 0),
                    add_next=lambda c: bias_rows(qi + 1, 0, c))

    _run_chains([chain(hh, mp) for hh in range(DIFF_HPS) for mp in range(2)],
                n_far_of(qi), (qi + 1) * steps_per_q, qi == 0, qi < pl.num_programs(2) - 1,
                n_far_of(qi + 1) == 0, s_scr, cm_scr, sh_scr, m_scr, acc_scr)

    lv = lam_ref[...]
    lam = (jnp.exp(jnp.sum(lv[0:1] * lv[1:2], axis=-1, keepdims=True))
           - jnp.exp(jnp.sum(lv[2:3] * lv[3:4], axis=-1, keepdims=True)) + LAMBDA_INIT)
    for hh in range(DIFF_HPS):
        a0, l0 = acc_scr[2 * hh, :DIFF_V], acc_scr[2 * hh, DIFF_V:DIFF_V + 1]
        a1, l1 = acc_scr[2 * hh + 1, :DIFF_V], acc_scr[2 * hh + 1, DIFF_V:DIFF_V + 1]
        o_t = a0 / l0 - lam * (a1 / l1)
        ms = jnp.sum(o_t * o_t, axis=0, keepdims=True) * (1.0 / DIFF_V)
        o_t = o_t * lax.rsqrt(ms + EPS)
        o_ref[0, :, hh * LANES:(hh + 1) * LANES] = (
            o_t.T * g_out_ref[...] * (1.0 - LAMBDA_INIT)).astype(o_ref.dtype)


def _diff_attention(rel_bias, qd, kd, vdt, bias, lam_vecs, g_out, b, s):
    t = TQ
    nq = s // t
    return pl.pallas_call(
        _diff_kernel,
        grid=(b, DIFF_HEADS // DIFF_HPS, nq),
        in_specs=[pl.BlockSpec(memory_space=pltpu.SMEM),
                  pl.BlockSpec((DIFF_HPS, t, LANES), lambda bi, hd, qi: (hd, bi * nq + qi, 0)),
                  pl.BlockSpec((DIFF_HPS, t, LANES),
                               lambda bi, hd, qi: (hd, bi * nq + jnp.minimum(qi + 1, nq - 1), 0)),
                  pl.BlockSpec((DIFF_HPS, s, LANES), lambda bi, hd, qi: (hd, bi, 0), pipeline_mode=RESIDENT),
                  pl.BlockSpec((1, s // TK, DIFF_HPS * DIFF_V, TK), lambda bi, hd, qi: (bi, 0, hd, 0),
                               pipeline_mode=RESIDENT),
                  pl.BlockSpec((DIFF_HPS, NEAR_KEYS + t, t), lambda bi, hd, qi: (hd, 0, 0),
                               pipeline_mode=RESIDENT),
                  pl.BlockSpec(lam_vecs.shape, lambda bi, hd, qi: (0, 0)),
                  pl.BlockSpec(g_out.shape, lambda bi, hd, qi: (0, 0))],
        out_specs=pl.BlockSpec((1, t, DIFF_HPS * LANES), lambda bi, hd, qi: (bi, qi, hd)),
        out_shape=jax.ShapeDtypeStruct((b, s, DIFF_HEADS * DIFF_V), BF16),
        scratch_shapes=_chain_scratch(2 * DIFF_HPS, DIFF_V),
        compiler_params=_params("parallel", "parallel", "arbitrary"),
        name="diff_attn",
    )(rel_bias, qd, qd, kd, vdt, bias, lam_vecs, g_out)


def _ffn_kernel(tiles_per_seq, x_ref, xh_ref, ym_ref, ymh_ref, yd_ref, ydh_ref, p_ref, wom_ref, wod_ref,
                g_ffn_ref, wg_ref, wu_ref, cw_ref, cb_ref, wd_ref, g_ple_ref, wpg_ref, wpp_ref, o_ref, g_scr):
    tm = x_ref.shape[0]
    ext = lambda halo_ref, ref: jnp.concatenate([halo_ref[...], ref[...]], axis=0)
    x1_ext = (ext(xh_ref, x_ref)
              + jnp.dot(ext(ymh_ref, ym_ref), wom_ref[...], preferred_element_type=F32)
              + jnp.dot(ext(ydh_ref, yd_ref), wod_ref[...], preferred_element_type=F32))
    h2_ext = (x1_ext * _rms(x1_ext, x1_ext.shape[-1]) * g_ffn_ref[...]).astype(BF16)
    g_scr[...] = jnp.dot(h2_ext, wg_ref[...], preferred_element_type=F32)

    @pl.when(pl.program_id(0) % tiles_per_seq == 0)
    def _():
        g_scr[0:HALO_ROWS, :] = jnp.zeros((HALO_ROWS, g_scr.shape[1]), F32)

    x1 = x1_ext[HALO_ROWS:]
    h2 = h2_ext[HALO_ROWS:]
    conv = cb_ref[...]
    for j in range(CONV_WIDTH):
        start = HALO_ROWS - (CONV_WIDTH - 1) + j
        conv = conv + g_scr[start:start + tm, :] * cw_ref[j:j + 1, :]
    up = jnp.dot(h2, wu_ref[...], preferred_element_type=F32)
    act = (conv * jax.nn.sigmoid(conv) * up).astype(BF16)
    x2 = x1 + jnp.dot(act, wd_ref[...], preferred_element_type=F32)
    hn = (x2 * _rms(x2, x2.shape[-1]) * g_ple_ref[...]).astype(BF16)
    gate = jax.nn.sigmoid(jnp.dot(hn, wpg_ref[...], preferred_element_type=F32))
    proj = jnp.dot(p_ref[...].astype(BF16), wpp_ref[...], preferred_element_type=F32)
    o_ref[...] = x2 + gate * proj


def _ffn_ple(x2, ym, yd, p2, w_out_m, w_out_d, g_ffn, w_gate, w_up, conv_w, conv_b, w_down, g_ple, w_pg, w_pp,
             seq):
    n, dm = x2.shape
    tm = min(FFN_ROW_TILE, seq)
    d_ff = w_gate.shape[1]
    row = lambda i: (i, 0)
    const = lambda i: (0, 0)
    halo = lambda i: (jnp.maximum(i * (tm // HALO_ROWS) - 1, 0), 0)

    def full(a):
        return pl.BlockSpec(a.shape, const)

    def tile_and_halo(a):
        return [pl.BlockSpec((tm, a.shape[1]), row), pl.BlockSpec((HALO_ROWS, a.shape[1]), halo)]

    return pl.pallas_call(
        functools.partial(_ffn_kernel, seq // tm),
        grid=(n // tm,),
        in_specs=(tile_and_halo(x2) + tile_and_halo(ym) + tile_and_halo(yd)
                  + [pl.BlockSpec((tm, p2.shape[1]), row), full(w_out_m), full(w_out_d), full(g_ffn),
                     full(w_gate), full(w_up), full(conv_w), full(conv_b), full(w_down),
                     full(g_ple), full(w_pg), full(w_pp)]),
        out_specs=pl.BlockSpec((tm, dm), row),
        out_shape=jax.ShapeDtypeStruct((n, dm), F32),
        scratch_shapes=[pltpu.VMEM((tm + HALO_ROWS, d_ff), F32)],
        compiler_params=_params("parallel"),
        name="ffn_ple",
    )(x2, x2, ym, ym, yd, yd, p2, w_out_m, w_out_d, g_ffn, w_gate, w_up, conv_w, conv_b, w_down,
      g_ple, w_pg, w_pp)


def _head_blocks(w, width, n_heads):
    k = w.shape[0]
    w3 = w.reshape(k, n_heads, width)
    return jnp.pad(w3, ((0, 0), (0, 0), (0, LANES - width))).reshape(k, n_heads * LANES)


def _swap_rope_halves(a):
    half = MLA_ROPE // 2
    return jnp.concatenate([a[..., :MLA_NOPE], a[..., MLA_NOPE + half:MLA_QK],
                            a[..., MLA_NOPE:MLA_NOPE + half]], axis=-1)


def _lane_row(g, width=LANES):
    return jnp.pad(g, (0, width - g.shape[0])).reshape(1, width).astype(F32)


def _rope_tables(seq):
    half = MLA_ROPE // 2
    inv_freq = ROPE_THETA ** (-jnp.arange(half, dtype=F32) / half)
    ang = jnp.arange(seq, dtype=jnp.int32).astype(F32)[:, None] * inv_freq[None, :]
    cos, sin = jnp.cos(ang), jnp.sin(ang)
    ones = jnp.ones((seq, MLA_NOPE), F32)
    zeros_n = jnp.zeros((seq, MLA_NOPE), F32)
    zeros_p = jnp.zeros((seq, LANES - MLA_QK), F32)
    cos_t = jnp.concatenate([ones, cos, cos, zeros_p], axis=1)
    sin_t = jnp.concatenate([zeros_n, -sin, sin, zeros_p], axis=1)
    return cos_t, sin_t


def kernel(x, p, attn_norm_g, w_in, q_lat_norm_g, w_uq, kv_lat_norm_g, w_ukv, mla_q_norm_g, mla_k_norm_g,
           diff_q_norm_g, diff_k_norm_g, lambda_q1, lambda_k1, lambda_q2, lambda_k2, diff_out_norm_g,
           rel_bias, w_out, ffn_norm_g, w_gate, w_up, conv_w, conv_b, w_down, ple_norm_g, w_ple_gate,
           w_ple_proj):
    b, s, dm = x.shape
    depth = p.shape[0]
    assert s % TQ == 0 and s % IN_ROW_TILE == 0 and s % FFN_ROW_TILE == 0
    assert TQ % KSTEP == 0 and NEAR_KEYS % KSTEP == 0 and IN_ROW_TILE % TK == 0
    assert depth == 1

    cos_t, sin_t = _rope_tables(s)
    bias, mask = _bias_tiles(rel_bias.astype(F32))
    x2 = x.reshape(b * s, dm)

    for i in range(depth):
        wi = w_in[i]
        off_kr = MLA_Q_RANK + MLA_KV_RANK
        off_dq = off_kr + MLA_ROPE
        off_dv = off_dq + 2 * DIFF_HEADS * DIFF_V
        k_rope = wi[:, off_kr:off_dq]
        half = MLA_ROPE // 2
        k_rope_sw = jnp.concatenate([k_rope[:, half:], k_rope[:, :half]], axis=1)
        lane_pad = ((0, 0), (MLA_NOPE, LANES - MLA_QK))
        w_in_p = jnp.concatenate([wi[:, :off_kr], jnp.pad(k_rope, lane_pad), jnp.pad(k_rope_sw, lane_pad),
                                  wi[:, off_dq:off_dv]], axis=1).astype(BF16)
        w_dvt = wi[:, off_dv:].T.astype(BF16)
        w_uq_p = _head_blocks(w_uq[i], MLA_QK, MLA_HEADS).astype(BF16)
        w_uq_sw = _head_blocks(
            _swap_rope_halves(w_uq[i].reshape(MLA_Q_RANK, MLA_HEADS, MLA_QK)).reshape(MLA_Q_RANK, -1),
            MLA_QK, MLA_HEADS).astype(BF16)
        w_ukv3 = w_ukv[i].reshape(MLA_KV_RANK, MLA_HEADS, MLA_NOPE + MLA_V)
        w_uk_p = _head_blocks(w_ukv3[:, :, :MLA_NOPE].reshape(MLA_KV_RANK, -1), MLA_NOPE, MLA_HEADS).astype(BF16)
        w_uvt = w_ukv3[:, :, MLA_NOPE:].reshape(MLA_KV_RANK, MLA_HEADS * MLA_V).T.astype(BF16)
        gq, gk = mla_q_norm_g[i], mla_k_norm_g[i]
        gdq = jnp.tile(diff_q_norm_g[i], 2).reshape(1, LANES).astype(F32)
        gdk = jnp.tile(diff_k_norm_g[i], 2).reshape(1, LANES).astype(F32)

        qm, km, vmt, qd, kd, vdt = _in_proj(
            x2, cos_t, sin_t, attn_norm_g[i].reshape(1, dm), w_in_p, w_dvt, q_lat_norm_g[i].reshape(1, -1),
            w_uq_p, w_uq_sw, kv_lat_norm_g[i].reshape(1, -1), w_uk_p, w_uvt,
            _lane_row(gq), _lane_row(_swap_rope_halves(gq)), _lane_row(gk), _lane_row(_swap_rope_halves(gk)),
            gdq, gdk, b, s)

        y_mla = _mla_attention(qm, km, vmt, mask, b, s)
        lam_vecs = jnp.stack([lambda_q1[i], lambda_k1[i], lambda_q2[i], lambda_k2[i]]).astype(F32)
        y_diff = _diff_attention(rel_bias.astype(F32), qd, kd, vdt, bias, lam_vecs,
                                 diff_out_norm_g[i].reshape(1, DIFF_V).astype(F32), b, s)

        n_mla = MLA_HEADS * MLA_V
        x2 = _ffn_ple(x2, y_mla.reshape(b * s, -1), y_diff.reshape(b * s, -1), p[i].reshape(b * s, -1),
                      w_out[i][:n_mla].astype(BF16), w_out[i][n_mla:].astype(BF16),
                      ffn_norm_g[i].reshape(1, dm), w_gate[i].astype(BF16), w_up[i].astype(BF16),
                      conv_w[i], conv_b[i].reshape(1, -1), w_down[i].astype(BF16),
                      ple_norm_g[i].reshape(1, dm), w_ple_gate[i].astype(BF16), w_ple_proj[i].astype(BF16), s)
    return x2.reshape(b, s, dm)
```

```python
import functools
import math

import jax
import jax.numpy as jnp
from jax import lax
from jax.experimental import pallas as pl
from jax.experimental.pallas import tpu as pltpu

F32 = jnp.float32
BF16 = jnp.bfloat16

LANES = 128
SUBLANES = 8
VMEM_LIMIT_BYTES = 56 * 1024 * 1024

CHUNK = 64
EPS = 1e-6
NEG_INF = -1e30
MLA_HEADS = 8
MLA_Q_RANK = 256
MLA_KV_RANK = 128
MLA_NOPE = 64
MLA_ROPE = 32
MLA_QK = MLA_NOPE + MLA_ROPE
MLA_V = 64
ROPE_THETA = 10000.0
DIFF_HEADS = 4
DIFF_QK = 64
DIFF_V = 2 * DIFF_QK
NUM_BUCKETS = 32
MAX_DISTANCE = 1024
CONV_WIDTH = 3
LAMBDA_INIT = 0.8 - 0.6 * math.exp(-0.3 * 0)
LOG2E = math.log2(math.e)

TQ = 512
TK = 512
KSPLIT = 1
KSTEP = TK * KSPLIT
SUM_ROWS = 16
NEAR_KEYS = 1024
MLA_HPS = 8
DIFF_HPS = 4
IN_ROW_TILE = 512
FFN_ROW_TILE = 512
HALO_ROWS = 2 * SUBLANES

ZC_QLAT = 0
ZC_KVLAT = ZC_QLAT + MLA_Q_RANK
ZC_KR = ZC_KVLAT + MLA_KV_RANK
ZC_KRSW = ZC_KR + LANES
ZC_DQ = ZC_KRSW + LANES
ZC_DK = ZC_DQ + DIFF_HEADS * DIFF_V
ZC_END = ZC_DK + DIFF_HEADS * DIFF_V

NT_DIMS = (((1,), (1,)), ((), ()))
RESIDENT = pl.Buffered(1)


def _params(*sem):
    return pltpu.CompilerParams(dimension_semantics=sem, vmem_limit_bytes=VMEM_LIMIT_BYTES)


def _rms(x, width):
    return lax.rsqrt(jnp.sum(x * x, axis=-1, keepdims=True) * (1.0 / width) + EPS)


def _bias_kernel(tab_ref, bias_ref, mask_ref):
    t = TQ
    kk = lax.broadcasted_iota(jnp.int32, (t, t), 0)
    qq = lax.broadcasted_iota(jnp.int32, (t, t), 1)
    chunk_bits = CHUNK.bit_length() - 1
    q_chunk = lax.shift_right_logical(qq, chunk_bits)
    mask_ref[...] = jnp.where(lax.shift_right_logical(kk, chunk_bits) <= q_chunk, 0.0, NEG_INF).astype(F32)
    key_off = kk + pl.program_id(0) * t - NEAR_KEYS
    add_mask = jnp.where(lax.shift_right_arithmetic(key_off, chunk_bits) <= q_chunk, 0.0, NEG_INF).astype(F32)
    rel = key_off - qq
    nb = NUM_BUCKETS // 2
    max_exact = nb // 2
    sign_off = (rel > 0).astype(jnp.int32) * nb
    n = jnp.abs(rel)
    nf = jnp.maximum(n, 1).astype(F32)
    large = max_exact + (jnp.log(nf / max_exact) / math.log(MAX_DISTANCE / max_exact)
                         * (nb - max_exact)).astype(jnp.int32)
    large = jnp.minimum(large, nb - 1)
    bucket = sign_off + jnp.where(n < max_exact, n, large)
    for h in range(DIFF_HEADS):
        acc = jnp.zeros((t, t), F32)
        for b in range(NUM_BUCKETS):
            acc = jnp.where(bucket == b, tab_ref[b, h], acc)
        bias_ref[h] = acc * LOG2E + add_mask


def _bias_tiles(rel_bias):
    t = TQ
    n_blocks = (NEAR_KEYS + TQ) // t
    return pl.pallas_call(
        _bias_kernel,
        grid=(n_blocks,),
        in_specs=[pl.BlockSpec(memory_space=pltpu.SMEM)],
        out_specs=[pl.BlockSpec((DIFF_HEADS, t, t), lambda d: (0, d, 0)),
                   pl.BlockSpec((t, t), lambda d: (0, 0))],
        out_shape=[jax.ShapeDtypeStruct((DIFF_HEADS, n_blocks * t, t), F32),
                   jax.ShapeDtypeStruct((t, t), F32)],
        compiler_params=_params("arbitrary"),
        name="bias_tiles",
    )(rel_bias)


def _in_proj_kernel(x_ref, cos_ref, sin_ref, g_attn_ref, w_in_ref, w_dvt_ref, g_ql_ref, w_uq_ref, w_uqsw_ref,
                    g_kvl_ref, w_uk_ref, w_uvt_ref, gq_ref, gqsw_ref, gk_ref, gksw_ref,
                    gdq_ref, gdk_ref,
                    qm_ref, km_ref, vmt_ref, qd_ref, kd_ref, vdt_ref):
    x = x_ref[...]
    h = (x * _rms(x, x.shape[-1]) * g_attn_ref[...]).astype(BF16)
    z = jnp.dot(h, w_in_ref[:, :ZC_DQ], preferred_element_type=F32)

    def store_transposed(dst_ref, w_t, act):
        v_t = lax.dot_general(w_t, act, NT_DIMS, preferred_element_type=F32).astype(BF16)
        for c in range(v_t.shape[1] // TK):
            dst_ref[0, c] = v_t[:, c * TK:(c + 1) * TK]

    q_lat = z[:, ZC_QLAT:ZC_KVLAT]
    kv_lat = z[:, ZC_KVLAT:ZC_KR]
    kr = z[:, ZC_KR:ZC_KRSW]
    krsw = z[:, ZC_KRSW:ZC_DQ]

    qln = (q_lat * _rms(q_lat, MLA_Q_RANK) * g_ql_ref[...]).astype(BF16)
    q = jnp.dot(qln, w_uq_ref[...], preferred_element_type=F32)
    qsw = jnp.dot(qln, w_uqsw_ref[...], preferred_element_type=F32)
    kvn = (kv_lat * _rms(kv_lat, MLA_KV_RANK) * g_kvl_ref[...]).astype(BF16)
    kn = jnp.dot(kvn, w_uk_ref[...], preferred_element_type=F32)
    store_transposed(vmt_ref, w_uvt_ref[...], kvn)
    z_d = jnp.dot(h, w_in_ref[:, ZC_DQ:], preferred_element_type=F32)
    store_transposed(vdt_ref, w_dvt_ref[...], h)

    cos = cos_ref[...]
    sin = sin_ref[...]
    q_scale = MLA_QK ** -0.5 * LOG2E
    q_cos = cos * (gq_ref[...] * q_scale)
    q_sin = sin * (gqsw_ref[...] * q_scale)
    k_cos = cos * gk_ref[...]
    k_sin = sin * gksw_ref[...]
    for hd in range(MLA_HEADS):
        sl = slice(hd * LANES, (hd + 1) * LANES)
        qh = q[:, sl]
        qm_ref[hd] = (_rms(qh, MLA_QK) * (qh * q_cos + qsw[:, sl] * q_sin)).astype(BF16)
        kh = kn[:, sl] + kr
        km_ref[hd] = (_rms(kh, MLA_QK) * (kh * k_cos + krsw * k_sin)).astype(BF16)

    lane = lax.broadcasted_iota(jnp.int32, (1, LANES), 1)
    first_map = lane < DIFF_QK
    d_scale = DIFF_QK ** -0.5 * LOG2E
    for hd in range(DIFF_HEADS):
        sl = slice(hd * LANES, (hd + 1) * LANES)
        for src, g_ref, dst, scale in ((ZC_DQ, gdq_ref, qd_ref, d_scale), (ZC_DK, gdk_ref, kd_ref, 1.0)):
            blk = z_d[:, src - ZC_DQ + hd * LANES: src - ZC_DQ + (hd + 1) * LANES]
            sq = blk * blk
            tot = jnp.sum(sq, axis=-1, keepdims=True)
            lo = jnp.sum(jnp.where(first_map, sq, 0.0), axis=-1, keepdims=True)
            ms = jnp.where(first_map, lo, tot - lo) * (1.0 / DIFF_QK)
            dst[hd] = (blk * lax.rsqrt(ms + EPS) * (g_ref[...] * scale)).astype(BF16)


def _in_proj(x2, cos_t, sin_t, g_attn, w_in_p, w_dvt, g_ql, w_uq_p, w_uq_sw, g_kvl, w_uk_p, w_uvt,
             gq, gqsw, gk, gksw, gdq, gdk, batch, seq):
    n, dm = x2.shape
    tm = IN_ROW_TILE
    tiles_per_seq = seq // tm
    row = lambda i: (i, 0)
    const = lambda i: (0, 0)
    pos = lambda i: (i % tiles_per_seq, 0)
    tile4 = lambda i: (i // tiles_per_seq, i % tiles_per_seq, 0, 0)
    kt = tm // TK

    def full(a):
        return pl.BlockSpec(a.shape, const)

    def heads_out(heads):
        return (pl.BlockSpec((heads, tm, LANES), lambda i: (0, i, 0)),
                jax.ShapeDtypeStruct((heads, n, LANES), BF16))

    def transposed_out(width):
        return (pl.BlockSpec((1, kt, width, TK), tile4),
                jax.ShapeDtypeStruct((batch, tiles_per_seq * kt, width, TK), BF16))

    outs = [heads_out(MLA_HEADS), heads_out(MLA_HEADS), transposed_out(MLA_HEADS * MLA_V),
            heads_out(DIFF_HEADS), heads_out(DIFF_HEADS), transposed_out(DIFF_HEADS * DIFF_V)]
    return pl.pallas_call(
        _in_proj_kernel,
        grid=(n // tm,),
        in_specs=[pl.BlockSpec((tm, dm), row), pl.BlockSpec((tm, LANES), pos), pl.BlockSpec((tm, LANES), pos),
                  full(g_attn), full(w_in_p), full(w_dvt), full(g_ql), full(w_uq_p), full(w_uq_sw),
                  full(g_kvl), full(w_uk_p), full(w_uvt), full(gq), full(gqsw), full(gk), full(gksw),
                  full(gdq), full(gdk)],
        out_specs=[o[0] for o in outs],
        out_shape=[o[1] for o in outs],
        compiler_params=_params("parallel"),
        name="in_proj",
    )(x2, cos_t, sin_t, g_attn, w_in_p, w_dvt, g_ql, w_uq_p, w_uq_sw, g_kvl, w_uk_p, w_uvt,
      gq, gqsw, gk, gksw, gdq, gdk)


def _col_max(s_tiles):
    return functools.reduce(jnp.maximum, [jnp.max(s, axis=0, keepdims=True) for s in s_tiles])


def _key_rows(step_idx, c):
    return pl.ds(pl.multiple_of(step_idx * KSTEP + c * TK, TK), TK)


def _run_chains(chains, n_plain, n_all, first, has_next, next_decorated, s_scr, cm_scr, sh_scr, m_scr, acc_scr):
    row = lax.broadcasted_iota(jnp.int32, (SUM_ROWS, TK), 0)
    ones_rows = jnp.where(row == 0, 1.0, 0.0).astype(BF16)
    use_shift = any(ch["shift"] is not None for ch in chains)

    def park(ci, s_tiles, add_tiles):
        shift = chains[ci]["shift"]
        if add_tiles is not None:
            s_tiles = [s + a for s, a in zip(s_tiles, add_tiles)]
        for c, s in enumerate(s_tiles):
            s_scr[ci, c] = s
        owed = shift if (shift is not None and add_tiles is None) else 0.0
        cm_scr[ci] = functools.reduce(jnp.maximum, [jnp.max(s, axis=0, keepdims=True) for s in s_tiles]) + owed
        if use_shift:
            sh_scr[ci] = jnp.zeros(sh_scr.shape[1:], F32) + owed

    def produce(ci, j, decorated):
        ch = chains[ci]
        park(ci, ch["qk"](j), [ch["add"](j, c) for c in range(KSPLIT)] if decorated else None)

    def produce_next(ci, decorated):
        ch = chains[ci]
        park(ci, ch["qk_next"](), [ch["add_next"](c) for c in range(KSPLIT)] if decorated else None)

    def consume(ci, j):
        vt = chains[ci]["vt"]
        m = m_scr[ci]
        m_new = jnp.maximum(m, cm_scr[ci])
        alpha = jnp.exp2(m - m_new)
        m_sub = m_new - sh_scr[ci] if use_shift else m_new
        acc = alpha * acc_scr[ci]
        for c, v_t in enumerate(vt(j)):
            p = jnp.exp2(s_scr[ci, c] - m_sub).astype(BF16)
            acc = acc + jnp.dot(jnp.concatenate([v_t, ones_rows], axis=0), p, preferred_element_type=F32)
        m_scr[ci] = m_new
        acc_scr[ci] = acc

    n_chains = len(chains)
    for ci in range(n_chains):
        m_scr[ci] = jnp.full(m_scr.shape[1:], NEG_INF, F32)
        acc_scr[ci] = jnp.zeros(acc_scr.shape[1:], F32)

    @pl.when(first)
    def _():
        for ci in range(n_chains):
            produce(ci, 0, True)

    def run(start, stop, decorated):
        def body(j, carry):
            for ci in range(n_chains):
                consume(ci, j)
                produce(ci, j + 1, decorated)
            return carry
        lax.fori_loop(start, stop, body, 0)

    switch = jnp.maximum(n_plain - 1, 0)
    run(0, switch, False)
    run(switch, n_all - 1, True)

    last_variants = [(jnp.logical_not(has_next), None)]
    if next_decorated is False:
        last_variants.append((has_next, False))
    else:
        last_variants += [(jnp.logical_and(has_next, jnp.logical_not(next_decorated)), False),
                          (jnp.logical_and(has_next, next_decorated), True)]
    for pred, kind in last_variants:
        @pl.when(pred)
        def _(kind=kind):
            for ci in range(n_chains):
                consume(ci, n_all - 1)
                if kind is not None:
                    produce_next(ci, kind)


def _chain_scratch(n_chains, dv):
    stat = pltpu.VMEM((n_chains, 1, TQ), F32)
    return [pltpu.VMEM((n_chains, KSPLIT, TK, TQ), F32), stat, stat, stat,
            pltpu.VMEM((n_chains, dv + SUM_ROWS, TQ), F32)]


def _mla_kernel(q_ref, qn_ref, k_ref, vt_ref, mask_ref, o_ref, s_scr, cm_scr, sh_scr, m_scr, acc_scr):
    qi = pl.program_id(2)
    steps_per_q = TQ // KSTEP
    n_all = (qi + 1) * steps_per_q

    def mask(j, c):
        return mask_ref[pl.ds(pl.multiple_of((j * KSPLIT + c) * TK - qi * TQ, TK), TK), :]

    def chain(hh):
        def scores(q, j):
            return [lax.dot_general(k_ref[hh, _key_rows(j, c), :], q, NT_DIMS, preferred_element_type=F32)
                    for c in range(KSPLIT)]

        def vt(j):
            return [vt_ref[0, j * KSPLIT + c, hh * MLA_V:(hh + 1) * MLA_V, :] for c in range(KSPLIT)]

        return dict(qk=lambda j: scores(q_ref[hh], j), vt=vt, add=mask, shift=None,
                    qk_next=lambda: scores(qn_ref[hh], 0), add_next=None)

    _run_chains([chain(hh) for hh in range(MLA_HPS)], qi * steps_per_q, n_all,
                qi == 0, qi < pl.num_programs(2) - 1, False, s_scr, cm_scr, sh_scr, m_scr, acc_scr)
    o_t = jnp.concatenate([acc_scr[hh, :MLA_V] / acc_scr[hh, MLA_V:MLA_V + 1] for hh in range(MLA_HPS)], axis=0)
    o_ref[0] = o_t.T.astype(o_ref.dtype)


def _mla_attention(qm, km, vmt, mask, b, s):
    t = TQ
    nq = s // t
    pairs = MLA_HEADS // MLA_HPS
    return pl.pallas_call(
        _mla_kernel,
        grid=(b, pairs, nq),
        in_specs=[pl.BlockSpec((MLA_HPS, t, LANES), lambda bi, hp, qi: (hp, bi * nq + qi, 0)),
                  pl.BlockSpec((MLA_HPS, t, LANES),
                               lambda bi, hp, qi: (hp, bi * nq + jnp.minimum(qi + 1, nq - 1), 0)),
                  pl.BlockSpec((MLA_HPS, s, LANES), lambda bi, hp, qi: (hp, bi, 0), pipeline_mode=RESIDENT),
                  pl.BlockSpec((1, s // TK, MLA_HPS * MLA_V, TK), lambda bi, hp, qi: (bi, 0, hp, 0),
                               pipeline_mode=RESIDENT),
                  pl.BlockSpec((t, t), lambda bi, hp, qi: (0, 0))],
        out_specs=pl.BlockSpec((1, t, MLA_HPS * MLA_V), lambda bi, hp, qi: (bi, qi, hp)),
        out_shape=jax.ShapeDtypeStruct((b, s, MLA_HEADS * MLA_V), BF16),
        scratch_shapes=_chain_scratch(MLA_HPS, MLA_V),
        compiler_params=_params("parallel", "parallel", "arbitrary"),
        name="mla_attn",
    )(qm, qm, km, vmt, mask)


def _diff_kernel(tab_ref, q_ref, qn_ref, k_ref, vt_ref, bias_ref, lam_ref, g_out_ref, o_ref,
                 s_scr, cm_scr, sh_scr, m_scr, acc_scr):
    hp = pl.program_id(1)
    qi = pl.program_id(2)
    lane = lax.broadcasted_iota(jnp.int32, (1, LANES), 1)
    far_bias = [tab_ref[NUM_BUCKETS // 2 - 1, hp * DIFF_HPS + hh] * LOG2E for hh in range(DIFF_HPS)]
    steps_per_q = TQ // KSTEP

    def n_far_of(tile):
        return jnp.maximum(tile * steps_per_q - NEAR_KEYS // KSTEP, 0)

    def chain(hh, mp):
        in_map = (lane >= DIFF_QK) if mp else (lane < DIFF_QK)

        def scores(q_tile_ref, j):
            q = q_tile_ref[hh]
            q = jnp.where(in_map, q, jnp.zeros_like(q))
            return [lax.dot_general(k_ref[hh, _key_rows(j, c), :], q, NT_DIMS, preferred_element_type=F32)
                    for c in range(KSPLIT)]

        def vt(j):
            return [vt_ref[0, j * KSPLIT + c, hh * DIFF_V:(hh + 1) * DIFF_V, :] for c in range(KSPLIT)]

        def bias_rows(tile, j, c):
            off = pl.multiple_of((j * KSPLIT + c) * TK - (tile * TQ - NEAR_KEYS), TK)
            return bias_ref[hh, pl.ds(off, TK), :]

        return dict(qk=lambda j: scores(q_ref, j), vt=vt, add=lambda j, c: bias_rows(qi, j, c),
                    shift=far_bias[hh], qk_next=lambda: scores(qn_ref, 0),
                    add_next=lambda c: bias_rows(qi + 1, 0, c))

    _run_chains([chain(hh, mp) for hh in range(DIFF_HPS) for mp in range(2)],
                n_far_of(qi), (qi + 1) * steps_per_q, qi == 0, qi < pl.num_programs(2) - 1,
                n_far_of(qi + 1) == 0, s_scr, cm_scr, sh_scr, m_scr, acc_scr)

    lv = lam_ref[...]
    lam = (jnp.exp(jnp.sum(lv[0:1] * lv[1:2], axis=-1, keepdims=True))
           - jnp.exp(jnp.sum(lv[2:3] * lv[3:4], axis=-1, keepdims=True)) + LAMBDA_INIT)
    for hh in range(DIFF_HPS):
        a0, l0 = acc_scr[2 * hh, :DIFF_V], acc_scr[2 * hh, DIFF_V:DIFF_V + 1]
        a1, l1 = acc_scr[2 * hh + 1, :DIFF_V], acc_scr[2 * hh + 1, DIFF_V:DIFF_V + 1]
        o_t = a0 / l0 - lam * (a1 / l1)
        ms = jnp.sum(o_t * o_t, axis=0, keepdims=True) * (1.0 / DIFF_V)
        o_t = o_t * lax.rsqrt(ms + EPS)
        o_ref[0, :, hh * LANES:(hh + 1) * LANES] = (
            o_t.T * g_out_ref[...] * (1.0 - LAMBDA_INIT)).astype(o_ref.dtype)


def _diff_attention(rel_bias, qd, kd, vdt, bias, lam_vecs, g_out, b, s):
    t = TQ
    nq = s // t
    return pl.pallas_call(
        _diff_kernel,
        grid=(b, DIFF_HEADS // DIFF_HPS, nq),
        in_specs=[pl.BlockSpec(memory_space=pltpu.SMEM),
                  pl.BlockSpec((DIFF_HPS, t, LANES), lambda bi, hd, qi: (hd, bi * nq + qi, 0)),
                  pl.BlockSpec((DIFF_HPS, t, LANES),
                               lambda bi, hd, qi: (hd, bi * nq + jnp.minimum(qi + 1, nq - 1), 0)),
                  pl.BlockSpec((DIFF_HPS, s, LANES), lambda bi, hd, qi: (hd, bi, 0), pipeline_mode=RESIDENT),
                  pl.BlockSpec((1, s // TK, DIFF_HPS * DIFF_V, TK), lambda bi, hd, qi: (bi, 0, hd, 0),
                               pipeline_mode=RESIDENT),
                  pl.BlockSpec((DIFF_HPS, NEAR_KEYS + t, t), lambda bi, hd, qi: (hd, 0, 0),
                               pipeline_mode=RESIDENT),
                  pl.BlockSpec(lam_vecs.shape, lambda bi, hd, qi: (0, 0)),
                  pl.BlockSpec(g_out.shape, lambda bi, hd, qi: (0, 0))],
        out_specs=pl.BlockSpec((1, t, DIFF_HPS * LANES), lambda bi, hd, qi: (bi, qi, hd)),
        out_shape=jax.ShapeDtypeStruct((b, s, DIFF_HEADS * DIFF_V), BF16),
        scratch_shapes=_chain_scratch(2 * DIFF_HPS, DIFF_V),
        compiler_params=_params("parallel", "parallel", "arbitrary"),
        name="diff_attn",
    )(rel_bias, qd, qd, kd, vdt, bias, lam_vecs, g_out)


def _ffn_kernel(tiles_per_seq, x_ref, xh_ref, ym_ref, ymh_ref, yd_ref, ydh_ref, p_ref, wom_ref, wod_ref,
                g_ffn_ref, wg_ref, wu_ref, cw_ref, cb_ref, wd_ref, g_ple_ref, wpg_ref, wpp_ref, o_ref, g_scr):
    tm = x_ref.shape[0]
    ext = lambda halo_ref, ref: jnp.concatenate([halo_ref[...], ref[...]], axis=0)
    x1_ext = (ext(xh_ref, x_ref)
              + jnp.dot(ext(ymh_ref, ym_ref), wom_ref[...], preferred_element_type=F32)
              + jnp.dot(ext(ydh_ref, yd_ref), wod_ref[...], preferred_element_type=F32))
    h2_ext = (x1_ext * _rms(x1_ext, x1_ext.shape[-1]) * g_ffn_ref[...]).astype(BF16)
    g_scr[...] = jnp.dot(h2_ext, wg_ref[...], preferred_element_type=F32)

    @pl.when(pl.program_id(0) % tiles_per_seq == 0)
    def _():
        g_scr[0:HALO_ROWS, :] = jnp.zeros((HALO_ROWS, g_scr.shape[1]), F32)

    x1 = x1_ext[HALO_ROWS:]
    h2 = h2_ext[HALO_ROWS:]
    conv = cb_ref[...]
    for j in range(CONV_WIDTH):
        start = HALO_ROWS - (CONV_WIDTH - 1) + j
        conv = conv + g_scr[start:start + tm, :] * cw_ref[j:j + 1, :]
    up = jnp.dot(h2, wu_ref[...], preferred_element_type=F32)
    act = (conv * jax.nn.sigmoid(conv) * up).astype(BF16)
    x2 = x1 + jnp.dot(act, wd_ref[...], preferred_element_type=F32)
    hn = (x2 * _rms(x2, x2.shape[-1]) * g_ple_ref[...]).astype(BF16)
    gate = jax.nn.sigmoid(jnp.dot(hn, wpg_ref[...], preferred_element_type=F32))
    proj = jnp.dot(p_ref[...].astype(BF16), wpp_ref[...], preferred_element_type=F32)
    o_ref[...] = x2 + gate * proj


def _ffn_ple(x2, ym, yd, p2, w_out_m, w_out_d, g_ffn, w_gate, w_up, conv_w, conv_b, w_down, g_ple, w_pg, w_pp,
             seq):
    n, dm = x2.shape
    tm = min(FFN_ROW_TILE, seq)
    d_ff = w_gate.shape[1]
    row = lambda i: (i, 0)
    const = lambda i: (0, 0)
    halo = lambda i: (jnp.maximum(i * (tm // HALO_ROWS) - 1, 0), 0)

    def full(a):
        return pl.BlockSpec(a.shape, const)

    def tile_and_halo(a):
        return [pl.BlockSpec((tm, a.shape[1]), row), pl.BlockSpec((HALO_ROWS, a.shape[1]), halo)]

    return pl.pallas_call(
        functools.partial(_ffn_kernel, seq // tm),
        grid=(n // tm,),
        in_specs=(tile_and_halo(x2) + tile_and_halo(ym) + tile_and_halo(yd)
                  + [pl.BlockSpec((tm, p2.shape[1]), row), full(w_out_m), full(w_out_d), full(g_ffn),
                     full(w_gate), full(w_up), full(conv_w), full(conv_b), full(w_down),
                     full(g_ple), full(w_pg), full(w_pp)]),
        out_specs=pl.BlockSpec((tm, dm), row),
        out_shape=jax.ShapeDtypeStruct((n, dm), F32),
        scratch_shapes=[pltpu.VMEM((tm + HALO_ROWS, d_ff), F32)],
        compiler_params=_params("parallel"),
        name="ffn_ple",
    )(x2, x2, ym, ym, yd, yd, p2, w_out_m, w_out_d, g_ffn, w_gate, w_up, conv_w, conv_b, w_down,
      g_ple, w_pg, w_pp)


def _head_blocks(w, width, n_heads):
    k = w.shape[0]
    w3 = w.reshape(k, n_heads, width)
    return jnp.pad(w3, ((0, 0), (0, 0), (0, LANES - width))).reshape(k, n_heads * LANES)


def _swap_rope_halves(a):
    half = MLA_ROPE // 2
    return jnp.concatenate([a[..., :MLA_NOPE], a[..., MLA_NOPE + half:MLA_QK],
                            a[..., MLA_NOPE:MLA_NOPE + half]], axis=-1)


def _lane_row(g, width=LANES):
    return jnp.pad(g, (0, width - g.shape[0])).reshape(1, width).astype(F32)


def _rope_tables(seq):
    half = MLA_ROPE // 2
    inv_freq = ROPE_THETA ** (-jnp.arange(half, dtype=F32) / half)
    ang = jnp.arange(seq, dtype=jnp.int32).astype(F32)[:, None] * inv_freq[None, :]
    cos, sin = jnp.cos(ang), jnp.sin(ang)
    ones = jnp.ones((seq, MLA_NOPE), F32)
    zeros_n = jnp.zeros((seq, MLA_NOPE), F32)
    zeros_p = jnp.zeros((seq, LANES - MLA_QK), F32)
    cos_t = jnp.concatenate([ones, cos, cos, zeros_p], axis=1)
    sin_t = jnp.concatenate([zeros_n, -sin, sin, zeros_p], axis=1)
    return cos_t, sin_t


def kernel(x, p, attn_norm_g, w_in, q_lat_norm_g, w_uq, kv_lat_norm_g, w_ukv, mla_q_norm_g, mla_k_norm_g,
           diff_q_norm_g, diff_k_norm_g, lambda_q1, lambda_k1, lambda_q2, lambda_k2, diff_out_norm_g,
           rel_bias, w_out, ffn_norm_g, w_gate, w_up, conv_w, conv_b, w_down, ple_norm_g, w_ple_gate,
           w_ple_proj):
    b, s, dm = x.shape
    depth = p.shape[0]
    assert s % TQ == 0 and s % IN_ROW_TILE == 0 and s % FFN_ROW_TILE == 0
    assert TQ % KSTEP == 0 and NEAR_KEYS % KSTEP == 0 and IN_ROW_TILE % TK == 0
    assert depth == 1

    cos_t, sin_t = _rope_tables(s)
    bias, mask = _bias_tiles(rel_bias.astype(F32))
    x2 = x.reshape(b * s, dm)

    for i in range(depth):
        wi = w_in[i]
        off_kr = MLA_Q_RANK + MLA_KV_RANK
        off_dq = off_kr + MLA_ROPE
        off_dv = off_dq + 2 * DIFF_HEADS * DIFF_V
        k_rope = wi[:, off_kr:off_dq]
        half = MLA_ROPE // 2
        k_rope_sw = jnp.concatenate([k_rope[:, half:], k_rope[:, :half]], axis=1)
        lane_pad = ((0, 0), (MLA_NOPE, LANES - MLA_QK))
        w_in_p = jnp.concatenate([wi[:, :off_kr], jnp.pad(k_rope, lane_pad), jnp.pad(k_rope_sw, lane_pad),
                                  wi[:, off_dq:off_dv]], axis=1).astype(BF16)
        w_dvt = wi[:, off_dv:].T.astype(BF16)
        w_uq_p = _head_blocks(w_uq[i], MLA_QK, MLA_HEADS).astype(BF16)
        w_uq_sw = _head_blocks(
            _swap_rope_halves(w_uq[i].reshape(MLA_Q_RANK, MLA_HEADS, MLA_QK)).reshape(MLA_Q_RANK, -1),
            MLA_QK, MLA_HEADS).astype(BF16)
        w_ukv3 = w_ukv[i].reshape(MLA_KV_RANK, MLA_HEADS, MLA_NOPE + MLA_V)
        w_uk_p = _head_blocks(w_ukv3[:, :, :MLA_NOPE].reshape(MLA_KV_RANK, -1), MLA_NOPE, MLA_HEADS).astype(BF16)
        w_uvt = w_ukv3[:, :, MLA_NOPE:].reshape(MLA_KV_RANK, MLA_HEADS * MLA_V).T.astype(BF16)
        gq, gk = mla_q_norm_g[i], mla_k_norm_g[i]
        gdq = jnp.tile(diff_q_norm_g[i], 2).reshape(1, LANES).astype(F32)
        gdk = jnp.tile(diff_k_norm_g[i], 2).reshape(1, LANES).astype(F32)

        qm, km, vmt, qd, kd, vdt = _in_proj(
            x2, cos_t, sin_t, attn_norm_g[i].reshape(1, dm), w_in_p, w_dvt, q_lat_norm_g[i].reshape(1, -1),
            w_uq_p, w_uq_sw, kv_lat_norm_g[i].reshape(1, -1), w_uk_p, w_uvt,
            _lane_row(gq), _lane_row(_swap_rope_halves(gq)), _lane_row(gk), _lane_row(_swap_rope_halves(gk)),
            gdq, gdk, b, s)

        y_mla = _mla_attention(qm, km, vmt, mask, b, s)
        lam_vecs = jnp.stack([lambda_q1[i], lambda_k1[i], lambda_q2[i], lambda_k2[i]]).astype(F32)
        y_diff = _diff_attention(rel_bias.astype(F32), qd, kd, vdt, bias, lam_vecs,
                                 diff_out_norm_g[i].reshape(1, DIFF_V).astype(F32), b, s)

        n_mla = MLA_HEADS * MLA_V
        x2 = _ffn_ple(x2, y_mla.reshape(b * s, -1), y_diff.reshape(b * s, -1), p[i].reshape(b * s, -1),
                      w_out[i][:n_mla].astype(BF16), w_out[i][n_mla:].astype(BF16),
                      ffn_norm_g[i].reshape(1, dm), w_gate[i].astype(BF16), w_up[i].astype(BF16),
                      conv_w[i], conv_b[i].reshape(1, -1), w_down[i].astype(BF16),
                      ple_norm_g[i].reshape(1, dm), w_ple_gate[i].astype(BF16), w_ple_proj[i].astype(BF16), s)
    return x2.reshape(b, s, dm)
```

```python
import functools
import math

import jax
import jax.numpy as jnp
from jax import lax
from jax.experimental import pallas as pl
from jax.experimental.pallas import tpu as pltpu

F32 = jnp.float32
BF16 = jnp.bfloat16

LANES = 128
SUBLANES = 8
VMEM_LIMIT_BYTES = 56 * 1024 * 1024

CHUNK = 64
EPS = 1e-6
NEG_INF = -1e30
MLA_HEADS = 8
MLA_Q_RANK = 256
MLA_KV_RANK = 128
MLA_NOPE = 64
MLA_ROPE = 32
MLA_QK = MLA_NOPE + MLA_ROPE
MLA_V = 64
ROPE_THETA = 10000.0
DIFF_HEADS = 4
DIFF_QK = 64
DIFF_V = 2 * DIFF_QK
NUM_BUCKETS = 32
MAX_DISTANCE = 1024
CONV_WIDTH = 3
LAMBDA_INIT = 0.8 - 0.6 * math.exp(-0.3 * 0)
LOG2E = math.log2(math.e)

TQ = 512
TK = 512
KSPLIT = 1
KSTEP = TK * KSPLIT
SUM_ROWS = 16
NEAR_KEYS = 1024
MLA_HPS = 8
DIFF_HPS = 4
IN_ROW_TILE = 512
FFN_ROW_TILE = 512
HALO_ROWS = 2 * SUBLANES

ZC_QLAT = 0
ZC_KVLAT = ZC_QLAT + MLA_Q_RANK
ZC_KR = ZC_KVLAT + MLA_KV_RANK
ZC_KRSW = ZC_KR + LANES
ZC_DQ = ZC_KRSW + LANES
ZC_DK = ZC_DQ + DIFF_HEADS * DIFF_V
ZC_END = ZC_DK + DIFF_HEADS * DIFF_V

NT_DIMS = (((1,), (1,)), ((), ()))
RESIDENT = pl.Buffered(1)


def _params(*sem):
    return pltpu.CompilerParams(dimension_semantics=sem, vmem_limit_bytes=VMEM_LIMIT_BYTES)


def _rms(x, width):
    return lax.rsqrt(jnp.sum(x * x, axis=-1, keepdims=True) * (1.0 / width) + EPS)


def _bias_kernel(tab_ref, bias_ref, mask_ref):
    t = TQ
    kk = lax.broadcasted_iota(jnp.int32, (t, t), 0)
    qq = lax.broadcasted_iota(jnp.int32, (t, t), 1)
    chunk_bits = CHUNK.bit_length() - 1
    q_chunk = lax.shift_right_logical(qq, chunk_bits)
    mask_ref[...] = jnp.where(lax.shift_right_logical(kk, chunk_bits) <= q_chunk, 0.0, NEG_INF).astype(F32)
    key_off = kk + pl.program_id(0) * t - NEAR_KEYS
    add_mask = jnp.where(lax.shift_right_arithmetic(key_off, chunk_bits) <= q_chunk, 0.0, NEG_INF).astype(F32)
    rel = key_off - qq
    nb = NUM_BUCKETS // 2
    max_exact = nb // 2
    sign_off = (rel > 0).astype(jnp.int32) * nb
    n = jnp.abs(rel)
    nf = jnp.maximum(n, 1).astype(F32)
    large = max_exact + jnp.floor(jnp.log(nf / max_exact) / math.log(MAX_DISTANCE / max_exact)
                                  * (nb - max_exact)).astype(jnp.int32)
    large = jnp.minimum(large, nb - 1)
    bucket = sign_off + jnp.where(n < max_exact, n, large)
    for h in range(DIFF_HEADS):
        acc = jnp.zeros((t, t), F32)
        for b in range(NUM_BUCKETS):
            acc = jnp.where(bucket == b, tab_ref[b, h], acc)
        bias_ref[h] = acc * LOG2E + add_mask


def _bias_tiles(rel_bias):
    t = TQ
    n_blocks = (NEAR_KEYS + TQ) // t
    return pl.pallas_call(
        _bias_kernel,
        grid=(n_blocks,),
        in_specs=[pl.BlockSpec(memory_space=pltpu.SMEM)],
        out_specs=[pl.BlockSpec((DIFF_HEADS, t, t), lambda d: (0, d, 0)),
                   pl.BlockSpec((t, t), lambda d: (0, 0))],
        out_shape=[jax.ShapeDtypeStruct((DIFF_HEADS, n_blocks * t, t), F32),
                   jax.ShapeDtypeStruct((t, t), F32)],
        compiler_params=_params("arbitrary"),
        name="bias_tiles",
    )(rel_bias)


def _in_proj_kernel(x_ref, cos_ref, sin_ref, g_attn_ref, w_in_ref, w_dvt_ref, g_ql_ref, w_uq_ref, w_uqsw_ref,
                    g_kvl_ref, w_uk_ref, w_uvt_ref, gq_ref, gqsw_ref, gk_ref, gksw_ref,
                    gdq_ref, gdk_ref,
                    qm_ref, km_ref, vmt_ref, qd_ref, kd_ref, vdt_ref):
    x = x_ref[...]
    h = (x * _rms(x, x.shape[-1]) * g_attn_ref[...]).astype(BF16)
    z = jnp.dot(h, w_in_ref[:, :ZC_DQ], preferred_element_type=F32)

    def store_transposed(dst_ref, w_t, act):
        v_t = lax.dot_general(w_t, act, NT_DIMS, preferred_element_type=F32).astype(BF16)
        for c in range(v_t.shape[1] // TK):
            dst_ref[0, c] = v_t[:, c * TK:(c + 1) * TK]

    q_lat = z[:, ZC_QLAT:ZC_KVLAT]
    kv_lat = z[:, ZC_KVLAT:ZC_KR]
    kr = z[:, ZC_KR:ZC_KRSW]
    krsw = z[:, ZC_KRSW:ZC_DQ]

    qln = (q_lat * _rms(q_lat, MLA_Q_RANK) * g_ql_ref[...]).astype(BF16)
    q = jnp.dot(qln, w_uq_ref[...], preferred_element_type=F32)
    qsw = jnp.dot(qln, w_uqsw_ref[...], preferred_element_type=F32)
    kvn = (kv_lat * _rms(kv_lat, MLA_KV_RANK) * g_kvl_ref[...]).astype(BF16)
    kn = jnp.dot(kvn, w_uk_ref[...], preferred_element_type=F32)
    store_transposed(vmt_ref, w_uvt_ref[...], kvn)
    z_d = jnp.dot(h, w_in_ref[:, ZC_DQ:], preferred_element_type=F32)
    store_transposed(vdt_ref, w_dvt_ref[...], h)

    cos = cos_ref[...]
    sin = sin_ref[...]
    q_scale = MLA_QK ** -0.5 * LOG2E
    q_cos = cos * (gq_ref[...] * q_scale)
    q_sin = sin * (gqsw_ref[...] * q_scale)
    k_cos = cos * gk_ref[...]
    k_sin = sin * gksw_ref[...]
    for hd in range(MLA_HEADS):
        sl = slice(hd * LANES, (hd + 1) * LANES)
        qh = q[:, sl]
        qm_ref[hd] = (_rms(qh, MLA_QK) * (qh * q_cos + qsw[:, sl] * q_sin)).astype(BF16)
        kh = kn[:, sl] + kr
        km_ref[hd] = (_rms(kh, MLA_QK) * (kh * k_cos + krsw * k_sin)).astype(BF16)

    lane = lax.broadcasted_iota(jnp.int32, (1, LANES), 1)
    first_map = lane < DIFF_QK
    d_scale = DIFF_QK ** -0.5 * LOG2E
    for hd in range(DIFF_HEADS):
        sl = slice(hd * LANES, (hd + 1) * LANES)
        for src, g_ref, dst, scale in ((ZC_DQ, gdq_ref, qd_ref, d_scale), (ZC_DK, gdk_ref, kd_ref, 1.0)):
            blk = z_d[:, src - ZC_DQ + hd * LANES: src - ZC_DQ + (hd + 1) * LANES]
            sq = blk * blk
            tot = jnp.sum(sq, axis=-1, keepdims=True)
            lo = jnp.sum(jnp.where(first_map, sq, 0.0), axis=-1, keepdims=True)
            ms = jnp.where(first_map, lo, tot - lo) * (1.0 / DIFF_QK)
            dst[hd] = (blk * lax.rsqrt(ms + EPS) * (g_ref[...] * scale)).astype(BF16)


def _in_proj(x2, cos_t, sin_t, g_attn, w_in_p, w_dvt, g_ql, w_uq_p, w_uq_sw, g_kvl, w_uk_p, w_uvt,
             gq, gqsw, gk, gksw, gdq, gdk, batch, seq):
    n, dm = x2.shape
    tm = IN_ROW_TILE
    tiles_per_seq = seq // tm
    row = lambda i: (i, 0)
    const = lambda i: (0, 0)
    pos = lambda i: (i % tiles_per_seq, 0)
    tile4 = lambda i: (i // tiles_per_seq, i % tiles_per_seq, 0, 0)
    kt = tm // TK

    def full(a):
        return pl.BlockSpec(a.shape, const)

    def heads_out(heads):
        return (pl.BlockSpec((heads, tm, LANES), lambda i: (0, i, 0)),
                jax.ShapeDtypeStruct((heads, n, LANES), BF16))

    def transposed_out(width):
        return (pl.BlockSpec((1, kt, width, TK), tile4),
                jax.ShapeDtypeStruct((batch, tiles_per_seq * kt, width, TK), BF16))

    outs = [heads_out(MLA_HEADS), heads_out(MLA_HEADS), transposed_out(MLA_HEADS * MLA_V),
            heads_out(DIFF_HEADS), heads_out(DIFF_HEADS), transposed_out(DIFF_HEADS * DIFF_V)]
    return pl.pallas_call(
        _in_proj_kernel,
        grid=(n // tm,),
        in_specs=[pl.BlockSpec((tm, dm), row), pl.BlockSpec((tm, LANES), pos), pl.BlockSpec((tm, LANES), pos),
                  full(g_attn), full(w_in_p), full(w_dvt), full(g_ql), full(w_uq_p), full(w_uq_sw),
                  full(g_kvl), full(w_uk_p), full(w_uvt), full(gq), full(gqsw), full(gk), full(gksw),
                  full(gdq), full(gdk)],
        out_specs=[o[0] for o in outs],
        out_shape=[o[1] for o in outs],
        compiler_params=_params("parallel"),
        name="in_proj",
    )(x2, cos_t, sin_t, g_attn, w_in_p, w_dvt, g_ql, w_uq_p, w_uq_sw, g_kvl, w_uk_p, w_uvt,
      gq, gqsw, gk, gksw, gdq, gdk)


def _col_max(s_tiles):
    return functools.reduce(jnp.maximum, [jnp.max(s, axis=0, keepdims=True) for s in s_tiles])


def _key_rows(step_idx, c):
    return pl.ds(pl.multiple_of(step_idx * KSTEP + c * TK, TK), TK)


def _run_chains(chains, n_plain, n_all, first, has_next, next_decorated, s_scr, cm_scr, sh_scr, m_scr, acc_scr):
    row = lax.broadcasted_iota(jnp.int32, (SUM_ROWS, TK), 0)
    ones_rows = jnp.where(row == 0, 1.0, 0.0).astype(BF16)
    use_shift = any(ch["shift"] is not None for ch in chains)

    def park(ci, s_tiles, add_tiles):
        shift = chains[ci]["shift"]
        if add_tiles is not None:
            s_tiles = [s + a for s, a in zip(s_tiles, add_tiles)]
        for c, s in enumerate(s_tiles):
            s_scr[ci, c] = s
        owed = shift if (shift is not None and add_tiles is None) else 0.0
        cm_scr[ci] = functools.reduce(jnp.maximum, [jnp.max(s, axis=0, keepdims=True) for s in s_tiles]) + owed
        if use_shift:
            sh_scr[ci] = jnp.zeros(sh_scr.shape[1:], F32) + owed

    def produce(ci, j, decorated):
        ch = chains[ci]
        park(ci, ch["qk"](j), [ch["add"](j, c) for c in range(KSPLIT)] if decorated else None)

    def produce_next(ci, decorated):
        ch = chains[ci]
        park(ci, ch["qk_next"](), [ch["add_next"](c) for c in range(KSPLIT)] if decorated else None)

    def consume(ci, j):
        vt = chains[ci]["vt"]
        m = m_scr[ci]
        m_new = jnp.maximum(m, cm_scr[ci])
        alpha = jnp.exp2(m - m_new)
        m_sub = m_new - sh_scr[ci] if use_shift else m_new
        acc = alpha * acc_scr[ci]
        for c, v_t in enumerate(vt(j)):
            p = jnp.exp2(s_scr[ci, c] - m_sub).astype(BF16)
            acc = acc + jnp.dot(jnp.concatenate([v_t, ones_rows], axis=0), p, preferred_element_type=F32)
        m_scr[ci] = m_new
        acc_scr[ci] = acc

    n_chains = len(chains)
    for ci in range(n_chains):
        m_scr[ci] = jnp.full(m_scr.shape[1:], NEG_INF, F32)
        acc_scr[ci] = jnp.zeros(acc_scr.shape[1:], F32)

    @pl.when(first)
    def _():
        for ci in range(n_chains):
            produce(ci, 0, True)

    def run(start, stop, decorated):
        def body(j, carry):
            for ci in range(n_chains):
                consume(ci, j)
                produce(ci, j + 1, decorated)
            return carry
        lax.fori_loop(start, stop, body, 0)

    switch = jnp.maximum(n_plain - 1, 0)
    run(0, switch, False)
    run(switch, n_all - 1, True)

    last_variants = [(jnp.logical_not(has_next), None)]
    if next_decorated is False:
        last_variants.append((has_next, False))
    else:
        last_variants += [(jnp.logical_and(has_next, jnp.logical_not(next_decorated)), False),
                          (jnp.logical_and(has_next, next_decorated), True)]
    for pred, kind in last_variants:
        @pl.when(pred)
        def _(kind=kind):
            for ci in range(n_chains):
                consume(ci, n_all - 1)
                if kind is not None:
                    produce_next(ci, kind)


def _chain_scratch(n_chains, dv):
    stat = pltpu.VMEM((n_chains, 1, TQ), F32)
    return [pltpu.VMEM((n_chains, KSPLIT, TK, TQ), F32), stat, stat, stat,
            pltpu.VMEM((n_chains, dv + SUM_ROWS, TQ), F32)]


def _mla_kernel(q_ref, qn_ref, k_ref, vt_ref, mask_ref, o_ref, s_scr, cm_scr, sh_scr, m_scr, acc_scr, qt_scr):
    qi = pl.program_id(2)
    steps_per_q = TQ // KSTEP
    n_all = (qi + 1) * steps_per_q

    def mask(j, c):
        return mask_ref[pl.ds(pl.multiple_of((j * KSPLIT + c) * TK - qi * TQ, TK), TK), :]

    for hh in range(MLA_HPS):
        qt_scr[0, hh] = q_ref[hh].T
        qt_scr[1, hh] = qn_ref[hh].T

    def chain(hh):
        def scores(which, j):
            return [jnp.dot(k_ref[hh, _key_rows(j, c), :], qt_scr[which, hh], preferred_element_type=F32)
                    for c in range(KSPLIT)]

        def vt(j):
            return [vt_ref[0, j * KSPLIT + c, hh * MLA_V:(hh + 1) * MLA_V, :] for c in range(KSPLIT)]

        return dict(qk=lambda j: scores(0, j), vt=vt, add=mask, shift=None,
                    qk_next=lambda: scores(1, 0), add_next=None)

    _run_chains([chain(hh) for hh in range(MLA_HPS)], qi * steps_per_q, n_all,
                qi == 0, qi < pl.num_programs(2) - 1, False, s_scr, cm_scr, sh_scr, m_scr, acc_scr)
    o_t = jnp.concatenate([acc_scr[hh, :MLA_V] / acc_scr[hh, MLA_V:MLA_V + 1] for hh in range(MLA_HPS)], axis=0)
    o_ref[0] = o_t.T.astype(o_ref.dtype)


def _mla_attention(qm, km, vmt, mask, b, s):
    t = TQ
    nq = s // t
    pairs = MLA_HEADS // MLA_HPS
    return pl.pallas_call(
        _mla_kernel,
        grid=(b, pairs, nq),
        in_specs=[pl.BlockSpec((MLA_HPS, t, LANES), lambda bi, hp, qi: (hp, bi * nq + qi, 0)),
                  pl.BlockSpec((MLA_HPS, t, LANES),
                               lambda bi, hp, qi: (hp, bi * nq + jnp.minimum(qi + 1, nq - 1), 0)),
                  pl.BlockSpec((MLA_HPS, s, LANES), lambda bi, hp, qi: (hp, bi, 0), pipeline_mode=RESIDENT),
                  pl.BlockSpec((1, s // TK, MLA_HPS * MLA_V, TK), lambda bi, hp, qi: (bi, 0, hp, 0),
                               pipeline_mode=RESIDENT),
                  pl.BlockSpec((t, t), lambda bi, hp, qi: (0, 0))],
        out_specs=pl.BlockSpec((1, t, MLA_HPS * MLA_V), lambda bi, hp, qi: (bi, qi, hp)),
        out_shape=jax.ShapeDtypeStruct((b, s, MLA_HEADS * MLA_V), BF16),
        scratch_shapes=_chain_scratch(MLA_HPS, MLA_V) + [pltpu.VMEM((2, MLA_HPS, LANES, TQ), BF16)],
        compiler_params=_params("parallel", "parallel", "arbitrary"),
        name="mla_attn",
    )(qm, qm, km, vmt, mask)


def _diff_kernel(tab_ref, q_ref, qn_ref, k_ref, vt_ref, bias_ref, lam_ref, g_out_ref, o_ref,
                 s_scr, cm_scr, sh_scr, m_scr, acc_scr, qt_scr):
    hp = pl.program_id(1)
    qi = pl.program_id(2)
    far_bias = [tab_ref[NUM_BUCKETS // 2 - 1, hp * DIFF_HPS + hh] * LOG2E for hh in range(DIFF_HPS)]
    steps_per_q = TQ // KSTEP

    def n_far_of(tile):
        return jnp.maximum(tile * steps_per_q - NEAR_KEYS // KSTEP, 0)

    feat = lax.broadcasted_iota(jnp.int32, (LANES, 1), 0)
    for which, ref in enumerate((q_ref, qn_ref)):
        for hh in range(DIFF_HPS):
            q_t = ref[hh].T
            zero = jnp.zeros_like(q_t)
            qt_scr[which, 2 * hh] = jnp.where(feat < DIFF_QK, q_t, zero)
            qt_scr[which, 2 * hh + 1] = jnp.where(feat >= DIFF_QK, q_t, zero)

    def chain(hh, mp):
        def scores(which, j):
            return [jnp.dot(k_ref[hh, _key_rows(j, c), :], qt_scr[which, 2 * hh + mp],
                            preferred_element_type=F32) for c in range(KSPLIT)]

        def vt(j):
            return [vt_ref[0, j * KSPLIT + c, hh * DIFF_V:(hh + 1) * DIFF_V, :] for c in range(KSPLIT)]

        def bias_rows(tile, j, c):
            off = pl.multiple_of((j * KSPLIT + c) * TK - (tile * TQ - NEAR_KEYS), TK)
            return bias_ref[hh, pl.ds(off, TK), :]

        return dict(qk=lambda j: scores(0, j), vt=vt, add=lambda j, c: bias_rows(qi, j, c),
                    shift=far_bias[hh], qk_next=lambda: scores(1, 0),
                    add_next=lambda c: bias_rows(qi + 1, 0, c))

    _run_chains([chain(hh, mp) for hh in range(DIFF_HPS) for mp in range(2)],
                n_far_of(qi), (qi + 1) * steps_per_q, qi == 0, qi < pl.num_programs(2) - 1,
                n_far_of(qi + 1) == 0, s_scr, cm_scr, sh_scr, m_scr, acc_scr)

    lv = lam_ref[...]
    lam = (jnp.exp(jnp.sum(lv[0:1] * lv[1:2], axis=-1, keepdims=True))
           - jnp.exp(jnp.sum(lv[2:3] * lv[3:4], axis=-1, keepdims=True)) + LAMBDA_INIT)
    for hh in range(DIFF_HPS):
        a0, l0 = acc_scr[2 * hh, :DIFF_V], acc_scr[2 * hh, DIFF_V:DIFF_V + 1]
        a1, l1 = acc_scr[2 * hh + 1, :DIFF_V], acc_scr[2 * hh + 1, DIFF_V:DIFF_V + 1]
        o_t = a0 / l0 - lam * (a1 / l1)
        ms = jnp.sum(o_t * o_t, axis=0, keepdims=True) * (1.0 / DIFF_V)
        o_t = o_t * lax.rsqrt(ms + EPS)
        o_ref[0, :, hh * LANES:(hh + 1) * LANES] = (
            o_t.T * g_out_ref[...] * (1.0 - LAMBDA_INIT)).astype(o_ref.dtype)


def _diff_attention(rel_bias, qd, kd, vdt, bias, lam_vecs, g_out, b, s):
    t = TQ
    nq = s // t
    return pl.pallas_call(
        _diff_kernel,
        grid=(b, DIFF_HEADS // DIFF_HPS, nq),
        in_specs=[pl.BlockSpec(memory_space=pltpu.SMEM),
                  pl.BlockSpec((DIFF_HPS, t, LANES), lambda bi, hd, qi: (hd, bi * nq + qi, 0)),
                  pl.BlockSpec((DIFF_HPS, t, LANES),
                               lambda bi, hd, qi: (hd, bi * nq + jnp.minimum(qi + 1, nq - 1), 0)),
                  pl.BlockSpec((DIFF_HPS, s, LANES), lambda bi, hd, qi: (hd, bi, 0), pipeline_mode=RESIDENT),
                  pl.BlockSpec((1, s // TK, DIFF_HPS * DIFF_V, TK), lambda bi, hd, qi: (bi, 0, hd, 0),
                               pipeline_mode=RESIDENT),
                  pl.BlockSpec((DIFF_HPS, NEAR_KEYS + t, t), lambda bi, hd, qi: (hd, 0, 0),
                               pipeline_mode=RESIDENT),
                  pl.BlockSpec(lam_vecs.shape, lambda bi, hd, qi: (0, 0)),
                  pl.BlockSpec(g_out.shape, lambda bi, hd, qi: (0, 0))],
        out_specs=pl.BlockSpec((1, t, DIFF_HPS * LANES), lambda bi, hd, qi: (bi, qi, hd)),
        out_shape=jax.ShapeDtypeStruct((b, s, DIFF_HEADS * DIFF_V), BF16),
        scratch_shapes=_chain_scratch(2 * DIFF_HPS, DIFF_V) + [pltpu.VMEM((2, 2 * DIFF_HPS, LANES, TQ), BF16)],
        compiler_params=_params("parallel", "parallel", "arbitrary"),
        name="diff_attn",
    )(rel_bias, qd, qd, kd, vdt, bias, lam_vecs, g_out)


def _ffn_kernel(tiles_per_seq, x_ref, xh_ref, ym_ref, ymh_ref, yd_ref, ydh_ref, p_ref, wom_ref, wod_ref,
                g_ffn_ref, wg_ref, wu_ref, cw_ref, cb_ref, wd_ref, g_ple_ref, wpg_ref, wpp_ref, o_ref, g_scr):
    tm = x_ref.shape[0]
    ext = lambda halo_ref, ref: jnp.concatenate([halo_ref[...], ref[...]], axis=0)
    x1_ext = (ext(xh_ref, x_ref)
              + jnp.dot(ext(ymh_ref, ym_ref), wom_ref[...], preferred_element_type=F32)
              + jnp.dot(ext(ydh_ref, yd_ref), wod_ref[...], preferred_element_type=F32))
    h2_ext = (x1_ext * _rms(x1_ext, x1_ext.shape[-1]) * g_ffn_ref[...]).astype(BF16)
    g_scr[...] = jnp.dot(h2_ext, wg_ref[...], preferred_element_type=F32)

    @pl.when(pl.program_id(0) % tiles_per_seq == 0)
    def _():
        g_scr[0:HALO_ROWS, :] = jnp.zeros((HALO_ROWS, g_scr.shape[1]), F32)

    x1 = x1_ext[HALO_ROWS:]
    h2 = h2_ext[HALO_ROWS:]
    conv = cb_ref[...]
    for j in range(CONV_WIDTH):
        start = HALO_ROWS - (CONV_WIDTH - 1) + j
        conv = conv + g_scr[start:start + tm, :] * cw_ref[j:j + 1, :]
    up = jnp.dot(h2, wu_ref[...], preferred_element_type=F32)
    act = (conv * jax.nn.sigmoid(conv) * up).astype(BF16)
    x2 = x1 + jnp.dot(act, wd_ref[...], preferred_element_type=F32)
    hn = (x2 * _rms(x2, x2.shape[-1]) * g_ple_ref[...]).astype(BF16)
    gate = jax.nn.sigmoid(jnp.dot(hn, wpg_ref[...], preferred_element_type=F32))
    proj = jnp.dot(p_ref[...].astype(BF16), wpp_ref[...], preferred_element_type=F32)
    o_ref[...] = x2 + gate * proj


def _ffn_ple(x2, ym, yd, p2, w_out_m, w_out_d, g_ffn, w_gate, w_up, conv_w, conv_b, w_down, g_ple, w_pg, w_pp,
             seq):
    n, dm = x2.shape
    tm = min(FFN_ROW_TILE, seq)
    d_ff = w_gate.shape[1]
    row = lambda i: (i, 0)
    const = lambda i: (0, 0)
    halo = lambda i: (jnp.maximum(i * (tm // HALO_ROWS) - 1, 0), 0)

    def full(a):
        return pl.BlockSpec(a.shape, const)

    def tile_and_halo(a):
        return [pl.BlockSpec((tm, a.shape[1]), row), pl.BlockSpec((HALO_ROWS, a.shape[1]), halo)]

    return pl.pallas_call(
        functools.partial(_ffn_kernel, seq // tm),
        grid=(n // tm,),
        in_specs=(tile_and_halo(x2) + tile_and_halo(ym) + tile_and_halo(yd)
                  + [pl.BlockSpec((tm, p2.shape[1]), row), full(w_out_m), full(w_out_d), full(g_ffn),
                     full(w_gate), full(w_up), full(conv_w), full(conv_b), full(w_down),
                     full(g_ple), full(w_pg), full(w_pp)]),
        out_specs=pl.BlockSpec((tm, dm), row),
        out_shape=jax.ShapeDtypeStruct((n, dm), F32),
        scratch_shapes=[pltpu.VMEM((tm + HALO_ROWS, d_ff), F32)],
        compiler_params=_params("parallel"),
        name="ffn_ple",
    )(x2, x2, ym, ym, yd, yd, p2, w_out_m, w_out_d, g_ffn, w_gate, w_up, conv_w, conv_b, w_down,
      g_ple, w_pg, w_pp)


def _head_blocks(w, width, n_heads):
    k = w.shape[0]
    w3 = w.reshape(k, n_heads, width)
    return jnp.pad(w3, ((0, 0), (0, 0), (0, LANES - width))).reshape(k, n_heads * LANES)


def _swap_rope_halves(a):
    half = MLA_ROPE // 2
    return jnp.concatenate([a[..., :MLA_NOPE], a[..., MLA_NOPE + half:MLA_QK],
                            a[..., MLA_NOPE:MLA_NOPE + half]], axis=-1)


def _lane_row(g, width=LANES):
    return jnp.pad(g, (0, width - g.shape[0])).reshape(1, width).astype(F32)


def _rope_tables(seq):
    half = MLA_ROPE // 2
    inv_freq = ROPE_THETA ** (-jnp.arange(half, dtype=F32) / half)
    ang = jnp.arange(seq, dtype=jnp.int32).astype(F32)[:, None] * inv_freq[None, :]
    cos, sin = jnp.cos(ang), jnp.sin(ang)
    ones = jnp.ones((seq, MLA_NOPE), F32)
    zeros_n = jnp.zeros((seq, MLA_NOPE), F32)
    zeros_p = jnp.zeros((seq, LANES - MLA_QK), F32)
    cos_t = jnp.concatenate([ones, cos, cos, zeros_p], axis=1)
    sin_t = jnp.concatenate([zeros_n, -sin, sin, zeros_p], axis=1)
    return cos_t, sin_t


def kernel(x, p, attn_norm_g, w_in, q_lat_norm_g, w_uq, kv_lat_norm_g, w_ukv, mla_q_norm_g, mla_k_norm_g,
           diff_q_norm_g, diff_k_norm_g, lambda_q1, lambda_k1, lambda_q2, lambda_k2, diff_out_norm_g,
           rel_bias, w_out, ffn_norm_g, w_gate, w_up, conv_w, conv_b, w_down, ple_norm_g, w_ple_gate,
           w_ple_proj):
    b, s, dm = x.shape
    depth = p.shape[0]
    assert s % TQ == 0 and s % IN_ROW_TILE == 0 and s % FFN_ROW_TILE == 0
    assert TQ % KSTEP == 0 and NEAR_KEYS % KSTEP == 0 and IN_ROW_TILE % TK == 0
    assert depth == 1

    cos_t, sin_t = _rope_tables(s)
    bias, mask = _bias_tiles(rel_bias.astype(F32))
    x2 = x.reshape(b * s, dm)

    for i in range(depth):
        wi = w_in[i]
        off_kr = MLA_Q_RANK + MLA_KV_RANK
        off_dq = off_kr + MLA_ROPE
        off_dv = off_dq + 2 * DIFF_HEADS * DIFF_V
        k_rope = wi[:, off_kr:off_dq]
        half = MLA_ROPE // 2
        k_rope_sw = jnp.concatenate([k_rope[:, half:], k_rope[:, :half]], axis=1)
        lane_pad = ((0, 0), (MLA_NOPE, LANES - MLA_QK))
        w_in_p = jnp.concatenate([wi[:, :off_kr], jnp.pad(k_rope, lane_pad), jnp.pad(k_rope_sw, lane_pad),
                                  wi[:, off_dq:off_dv]], axis=1).astype(BF16)
        w_dvt = wi[:, off_dv:].T.astype(BF16)
        w_uq_p = _head_blocks(w_uq[i], MLA_QK, MLA_HEADS).astype(BF16)
        w_uq_sw = _head_blocks(
            _swap_rope_halves(w_uq[i].reshape(MLA_Q_RANK, MLA_HEADS, MLA_QK)).reshape(MLA_Q_RANK, -1),
            MLA_QK, MLA_HEADS).astype(BF16)
        w_ukv3 = w_ukv[i].reshape(MLA_KV_RANK, MLA_HEADS, MLA_NOPE + MLA_V)
        w_uk_p = _head_blocks(w_ukv3[:, :, :MLA_NOPE].reshape(MLA_KV_RANK, -1), MLA_NOPE, MLA_HEADS).astype(BF16)
        w_uvt = w_ukv3[:, :, MLA_NOPE:].reshape(MLA_KV_RANK, MLA_HEADS * MLA_V).T.astype(BF16)
        gq, gk = mla_q_norm_g[i], mla_k_norm_g[i]
        gdq = jnp.tile(diff_q_norm_g[i], 2).reshape(1, LANES).astype(F32)
        gdk = jnp.tile(diff_k_norm_g[i], 2).reshape(1, LANES).astype(F32)

        qm, km, vmt, qd, kd, vdt = _in_proj(
            x2, cos_t, sin_t, attn_norm_g[i].reshape(1, dm), w_in_p, w_dvt, q_lat_norm_g[i].reshape(1, -1),
            w_uq_p, w_uq_sw, kv_lat_norm_g[i].reshape(1, -1), w_uk_p, w_uvt,
            _lane_row(gq), _lane_row(_swap_rope_halves(gq)), _lane_row(gk), _lane_row(_swap_rope_halves(gk)),
            gdq, gdk, b, s)

        y_mla = _mla_attention(qm, km, vmt, mask, b, s)
        lam_vecs = jnp.stack([lambda_q1[i], lambda_k1[i], lambda_q2[i], lambda_k2[i]]).astype(F32)
        y_diff = _diff_attention(rel_bias.astype(F32), qd, kd, vdt, bias, lam_vecs,
                                 diff_out_norm_g[i].reshape(1, DIFF_V).astype(F32), b, s)

        n_mla = MLA_HEADS * MLA_V
        x2 = _ffn_ple(x2, y_mla.reshape(b * s, -1), y_diff.reshape(b * s, -1), p[i].reshape(b * s, -1),
                      w_out[i][:n_mla].astype(BF16), w_out[i][n_mla:].astype(BF16),
                      ffn_norm_g[i].reshape(1, dm), w_gate[i].astype(BF16), w_up[i].astype(BF16),
                      conv_w[i], conv_b[i].reshape(1, -1), w_down[i].astype(BF16),
                      ple_norm_g[i].reshape(1, dm), w_ple_gate[i].astype(BF16), w_ple_proj[i].astype(BF16), s)
    return x2.reshape(b, s, dm)
```

```python
import functools
import math

import jax
import jax.numpy as jnp
from jax import lax
from jax.experimental import pallas as pl
from jax.experimental.pallas import tpu as pltpu

F32 = jnp.float32
BF16 = jnp.bfloat16

LANES = 128
SUBLANES = 8
VMEM_LIMIT_BYTES = 56 * 1024 * 1024

CHUNK = 64
EPS = 1e-6
NEG_INF = -1e30
MLA_HEADS = 8
MLA_Q_RANK = 256
MLA_KV_RANK = 128
MLA_NOPE = 64
MLA_ROPE = 32
MLA_QK = MLA_NOPE + MLA_ROPE
MLA_V = 64
ROPE_THETA = 10000.0
DIFF_HEADS = 4
DIFF_QK = 64
DIFF_V = 2 * DIFF_QK
NUM_BUCKETS = 32
MAX_DISTANCE = 1024
CONV_WIDTH = 3
LAMBDA_INIT = 0.8 - 0.6 * math.exp(-0.3 * 0)
LOG2E = math.log2(math.e)

TQ = 512
TK = 512
KSPLIT = 1
KSTEP = TK * KSPLIT
SUM_ROWS = 16
NEAR_KEYS = 1024
MLA_HPS = 8
DIFF_HPS = 4
IN_ROW_TILE = 512
FFN_ROW_TILE = 512
HALO_ROWS = 2 * SUBLANES

ZC_QLAT = 0
ZC_KVLAT = ZC_QLAT + MLA_Q_RANK
ZC_KR = ZC_KVLAT + MLA_KV_RANK
ZC_KRSW = ZC_KR + LANES
ZC_DQ = ZC_KRSW + LANES
ZC_DK = ZC_DQ + DIFF_HEADS * DIFF_V
ZC_END = ZC_DK + DIFF_HEADS * DIFF_V

NT_DIMS = (((1,), (1,)), ((), ()))


def _params(*sem):
    return pltpu.CompilerParams(dimension_semantics=sem, vmem_limit_bytes=VMEM_LIMIT_BYTES)


def _rms(x, width):
    return lax.rsqrt(jnp.sum(x * x, axis=-1, keepdims=True) * (1.0 / width) + EPS)


def _bias_kernel(tab_ref, bias_ref, mask_ref):
    t = TQ
    kk = lax.broadcasted_iota(jnp.int32, (t, t), 0)
    qq = lax.broadcasted_iota(jnp.int32, (t, t), 1)
    chunk_bits = CHUNK.bit_length() - 1
    q_chunk = lax.shift_right_logical(qq, chunk_bits)
    mask_ref[...] = jnp.where(lax.shift_right_logical(kk, chunk_bits) <= q_chunk, 0.0, NEG_INF).astype(F32)
    key_off = kk + pl.program_id(0) * t - NEAR_KEYS
    add_mask = jnp.where(lax.shift_right_arithmetic(key_off, chunk_bits) <= q_chunk, 0.0, NEG_INF).astype(F32)
    rel = key_off - qq
    nb = NUM_BUCKETS // 2
    max_exact = nb // 2
    sign_off = (rel > 0).astype(jnp.int32) * nb
    n = jnp.abs(rel)
    nf = jnp.maximum(n, 1).astype(F32)
    large = max_exact + jnp.floor(jnp.log(nf / max_exact) / math.log(MAX_DISTANCE / max_exact)
                                  * (nb - max_exact)).astype(jnp.int32)
    large = jnp.minimum(large, nb - 1)
    bucket = sign_off + jnp.where(n < max_exact, n, large)
    for h in range(DIFF_HEADS):
        acc = jnp.zeros((t, t), F32)
        for b in range(NUM_BUCKETS):
            acc = jnp.where(bucket == b, tab_ref[b, h], acc)
        bias_ref[h] = acc * LOG2E + add_mask


def _bias_tiles(rel_bias):
    t = TQ
    n_blocks = (NEAR_KEYS + TQ) // t
    return pl.pallas_call(
        _bias_kernel,
        grid=(n_blocks,),
        in_specs=[pl.BlockSpec(memory_space=pltpu.SMEM)],
        out_specs=[pl.BlockSpec((DIFF_HEADS, t, t), lambda d: (0, d, 0)),
                   pl.BlockSpec((t, t), lambda d: (0, 0))],
        out_shape=[jax.ShapeDtypeStruct((DIFF_HEADS, n_blocks * t, t), F32),
                   jax.ShapeDtypeStruct((t, t), F32)],
        compiler_params=_params("arbitrary"),
        name="bias_tiles",
    )(rel_bias)


def _in_proj_kernel(x_ref, cos_ref, sin_ref, g_attn_ref, w_in_ref, w_dvt_ref, g_ql_ref, w_uq_ref, w_uqsw_ref,
                    g_kvl_ref, w_uk_ref, w_uvt_ref, gq_ref, gqsw_ref, gk_ref, gksw_ref,
                    gdq_ref, gdk_ref,
                    qm_ref, km_ref, vmt_ref, qd_ref, kd_ref, vdt_ref):
    x = x_ref[...]
    h = (x * _rms(x, x.shape[-1]) * g_attn_ref[...]).astype(BF16)
    z = jnp.dot(h, w_in_ref[:, :ZC_DQ], preferred_element_type=F32)

    def store_transposed(dst_ref, w_t, act):
        v_t = lax.dot_general(w_t, act, NT_DIMS, preferred_element_type=F32).astype(BF16)
        for c in range(v_t.shape[1] // TK):
            dst_ref[0, c] = v_t[:, c * TK:(c + 1) * TK]

    q_lat = z[:, ZC_QLAT:ZC_KVLAT]
    kv_lat = z[:, ZC_KVLAT:ZC_KR]
    kr = z[:, ZC_KR:ZC_KRSW]
    krsw = z[:, ZC_KRSW:ZC_DQ]

    qln = (q_lat * _rms(q_lat, MLA_Q_RANK) * g_ql_ref[...]).astype(BF16)
    q = jnp.dot(qln, w_uq_ref[...], preferred_element_type=F32)
    qsw = jnp.dot(qln, w_uqsw_ref[...], preferred_element_type=F32)
    kvn = (kv_lat * _rms(kv_lat, MLA_KV_RANK) * g_kvl_ref[...]).astype(BF16)
    kn = jnp.dot(kvn, w_uk_ref[...], preferred_element_type=F32)
    store_transposed(vmt_ref, w_uvt_ref[...], kvn)
    z_d = jnp.dot(h, w_in_ref[:, ZC_DQ:], preferred_element_type=F32)
    store_transposed(vdt_ref, w_dvt_ref[...], h)

    cos = cos_ref[...]
    sin = sin_ref[...]
    q_scale = MLA_QK ** -0.5 * LOG2E
    q_cos = cos * (gq_ref[...] * q_scale)
    q_sin = sin * (gqsw_ref[...] * q_scale)
    k_cos = cos * gk_ref[...]
    k_sin = sin * gksw_ref[...]
    for hd in range(MLA_HEADS):
        sl = slice(hd * LANES, (hd + 1) * LANES)
        qh = q[:, sl]
        qm_ref[hd] = (_rms(qh, MLA_QK) * (qh * q_cos + qsw[:, sl] * q_sin)).astype(BF16)
        kh = kn[:, sl] + kr
        km_ref[hd] = (_rms(kh, MLA_QK) * (kh * k_cos + krsw * k_sin)).astype(BF16)

    lane = lax.broadcasted_iota(jnp.int32, (1, LANES), 1)
    first_map = lane < DIFF_QK
    d_scale = DIFF_QK ** -0.5 * LOG2E
    for hd in range(DIFF_HEADS):
        sl = slice(hd * LANES, (hd + 1) * LANES)
        for src, g_ref, dst, scale in ((ZC_DQ, gdq_ref, qd_ref, d_scale), (ZC_DK, gdk_ref, kd_ref, 1.0)):
            blk = z_d[:, src - ZC_DQ + hd * LANES: src - ZC_DQ + (hd + 1) * LANES]
            sq = blk * blk
            tot = jnp.sum(sq, axis=-1, keepdims=True)
            lo = jnp.sum(jnp.where(first_map, sq, 0.0), axis=-1, keepdims=True)
            ms = jnp.where(first_map, lo, tot - lo) * (1.0 / DIFF_QK)
            dst[hd] = (blk * lax.rsqrt(ms + EPS) * (g_ref[...] * scale)).astype(BF16)


def _in_proj(x2, cos_t, sin_t, g_attn, w_in_p, w_dvt, g_ql, w_uq_p, w_uq_sw, g_kvl, w_uk_p, w_uvt,
             gq, gqsw, gk, gksw, gdq, gdk, batch, seq):
    n, dm = x2.shape
    tm = IN_ROW_TILE
    tiles_per_seq = seq // tm
    row = lambda i: (i, 0)
    const = lambda i: (0, 0)
    pos = lambda i: (i % tiles_per_seq, 0)
    tile4 = lambda i: (i // tiles_per_seq, i % tiles_per_seq, 0, 0)
    kt = tm // TK

    def full(a):
        return pl.BlockSpec(a.shape, const)

    def heads_out(heads):
        return (pl.BlockSpec((heads, tm, LANES), lambda i: (0, i, 0)),
                jax.ShapeDtypeStruct((heads, n, LANES), BF16))

    def transposed_out(width):
        return (pl.BlockSpec((1, kt, width, TK), tile4),
                jax.ShapeDtypeStruct((batch, tiles_per_seq * kt, width, TK), BF16))

    outs = [heads_out(MLA_HEADS), heads_out(MLA_HEADS), transposed_out(MLA_HEADS * MLA_V),
            heads_out(DIFF_HEADS), heads_out(DIFF_HEADS), transposed_out(DIFF_HEADS * DIFF_V)]
    return pl.pallas_call(
        _in_proj_kernel,
        grid=(n // tm,),
        in_specs=[pl.BlockSpec((tm, dm), row), pl.BlockSpec((tm, LANES), pos), pl.BlockSpec((tm, LANES), pos),
                  full(g_attn), full(w_in_p), full(w_dvt), full(g_ql), full(w_uq_p), full(w_uq_sw),
                  full(g_kvl), full(w_uk_p), full(w_uvt), full(gq), full(gqsw), full(gk), full(gksw),
                  full(gdq), full(gdk)],
        out_specs=[o[0] for o in outs],
        out_shape=[o[1] for o in outs],
        compiler_params=_params("parallel"),
        name="in_proj",
    )(x2, cos_t, sin_t, g_attn, w_in_p, w_dvt, g_ql, w_uq_p, w_uq_sw, g_kvl, w_uk_p, w_uvt,
      gq, gqsw, gk, gksw, gdq, gdk)


def _col_max(s_tiles):
    return functools.reduce(jnp.maximum, [jnp.max(s, axis=0, keepdims=True) for s in s_tiles])


def _key_rows(step_idx, c):
    return pl.ds(pl.multiple_of(step_idx * KSTEP + c * TK, TK), TK)


def _run_chains(chains, n_plain, n_all, first, has_next, next_decorated, s_scr, cm_scr, sh_scr, m_scr, acc_scr):
    row = lax.broadcasted_iota(jnp.int32, (SUM_ROWS, TK), 0)
    ones_rows = jnp.where(row == 0, 1.0, 0.0).astype(BF16)
    use_shift = any(ch["shift"] is not None for ch in chains)

    def park(ci, s_tiles, add_tiles):
        shift = chains[ci]["shift"]
        if add_tiles is not None:
            s_tiles = [s + a for s, a in zip(s_tiles, add_tiles)]
        for c, s in enumerate(s_tiles):
            s_scr[ci, c] = s
        owed = shift if (shift is not None and add_tiles is None) else 0.0
        cm_scr[ci] = functools.reduce(jnp.maximum, [jnp.max(s, axis=0, keepdims=True) for s in s_tiles]) + owed
        if use_shift:
            sh_scr[ci] = jnp.zeros(sh_scr.shape[1:], F32) + owed

    def produce(ci, j, decorated):
        ch = chains[ci]
        park(ci, ch["qk"](j), [ch["add"](j, c) for c in range(KSPLIT)] if decorated else None)

    def produce_next(ci, decorated):
        ch = chains[ci]
        park(ci, ch["qk_next"](), [ch["add_next"](c) for c in range(KSPLIT)] if decorated else None)

    def consume(ci, j):
        vt = chains[ci]["vt"]
        m = m_scr[ci]
        m_new = jnp.maximum(m, cm_scr[ci])
        alpha = jnp.exp2(m - m_new)
        m_sub = m_new - sh_scr[ci] if use_shift else m_new
        acc = alpha * acc_scr[ci]
        for c, v_t in enumerate(vt(j)):
            p = jnp.exp2(s_scr[ci, c] - m_sub).astype(BF16)
            acc = acc + jnp.dot(jnp.concatenate([v_t, ones_rows], axis=0), p, preferred_element_type=F32)
        m_scr[ci] = m_new
        acc_scr[ci] = acc

    n_chains = len(chains)
    for ci in range(n_chains):
        m_scr[ci] = jnp.full(m_scr.shape[1:], NEG_INF, F32)
        acc_scr[ci] = jnp.zeros(acc_scr.shape[1:], F32)

    @pl.when(first)
    def _():
        for ci in range(n_chains):
            produce(ci, 0, True)

    def run(start, stop, decorated):
        def body(j, carry):
            for ci in range(n_chains):
                consume(ci, j)
                produce(ci, j + 1, decorated)
            return carry
        lax.fori_loop(start, stop, body, 0)

    switch = jnp.maximum(n_plain - 1, 0)
    run(0, switch, False)
    run(switch, n_all - 1, True)

    last_variants = [(jnp.logical_not(has_next), None)]
    if next_decorated is False:
        last_variants.append((has_next, False))
    else:
        last_variants += [(jnp.logical_and(has_next, jnp.logical_not(next_decorated)), False),
                          (jnp.logical_and(has_next, next_decorated), True)]
    for pred, kind in last_variants:
        @pl.when(pred)
        def _(kind=kind):
            for ci in range(n_chains):
                consume(ci, n_all - 1)
                if kind is not None:
                    produce_next(ci, kind)


def _chain_scratch(n_chains, dv):
    stat = pltpu.VMEM((n_chains, 1, TQ), F32)
    return [pltpu.VMEM((n_chains, KSPLIT, TK, TQ), F32), stat, stat, stat,
            pltpu.VMEM((n_chains, dv + SUM_ROWS, TQ), F32)]


def _stream_keys(qi, n_tiles, chunk_copies):
    @pl.when(qi == 0)
    def _():
        for cp in chunk_copies(0):
            cp.start()

    for cp in chunk_copies(qi):
        cp.wait()

    @pl.when(qi + 1 < n_tiles)
    def _():
        for cp in chunk_copies(qi + 1):
            cp.start()


def _mla_kernel(q_ref, qn_ref, k_hbm, vt_hbm, mask_ref, o_ref, s_scr, cm_scr, sh_scr, m_scr, acc_scr, qt_scr,
                k_ref, vt_ref, dma_sem):
    bi = pl.program_id(0)
    hp = pl.program_id(1)
    qi = pl.program_id(2)
    steps_per_q = TQ // KSTEP
    n_all = (qi + 1) * steps_per_q
    seq = k_ref.shape[1]
    vt_per_q = TQ // TK

    def chunk_copies(c):
        heads = pl.ds(hp * MLA_HPS, MLA_HPS)
        return [pltpu.make_async_copy(k_hbm.at[heads, pl.ds(bi * seq + c * TQ, TQ), :],
                                      k_ref.at[:, pl.ds(c * TQ, TQ), :], dma_sem.at[0]),
                pltpu.make_async_copy(vt_hbm.at[bi, pl.ds(c * vt_per_q, vt_per_q),
                                                pl.ds(hp * MLA_HPS * MLA_V, MLA_HPS * MLA_V), :],
                                      vt_ref.at[pl.ds(c * vt_per_q, vt_per_q)], dma_sem.at[1])]

    _stream_keys(qi, pl.num_programs(2), chunk_copies)

    def mask(j, c):
        return mask_ref[pl.ds(pl.multiple_of((j * KSPLIT + c) * TK - qi * TQ, TK), TK), :]

    for hh in range(MLA_HPS):
        qt_scr[0, hh] = q_ref[hh].T
        qt_scr[1, hh] = qn_ref[hh].T

    def chain(hh):
        def scores(which, j):
            return [jnp.dot(k_ref[hh, _key_rows(j, c), :], qt_scr[which, hh], preferred_element_type=F32)
                    for c in range(KSPLIT)]

        def vt(j):
            return [vt_ref[j * KSPLIT + c, hh * MLA_V:(hh + 1) * MLA_V, :] for c in range(KSPLIT)]

        return dict(qk=lambda j: scores(0, j), vt=vt, add=mask, shift=None,
                    qk_next=lambda: scores(1, 0), add_next=None)

    _run_chains([chain(hh) for hh in range(MLA_HPS)], qi * steps_per_q, n_all,
                qi == 0, qi < pl.num_programs(2) - 1, False, s_scr, cm_scr, sh_scr, m_scr, acc_scr)
    o_t = jnp.concatenate([acc_scr[hh, :MLA_V] / acc_scr[hh, MLA_V:MLA_V + 1] for hh in range(MLA_HPS)], axis=0)
    o_ref[0] = o_t.T.astype(o_ref.dtype)


def _mla_attention(qm, km, vmt, mask, b, s):
    t = TQ
    nq = s // t
    pairs = MLA_HEADS // MLA_HPS
    return pl.pallas_call(
        _mla_kernel,
        grid=(b, pairs, nq),
        in_specs=[pl.BlockSpec((MLA_HPS, t, LANES), lambda bi, hp, qi: (hp, bi * nq + qi, 0)),
                  pl.BlockSpec((MLA_HPS, t, LANES),
                               lambda bi, hp, qi: (hp, bi * nq + jnp.minimum(qi + 1, nq - 1), 0)),
                  pl.BlockSpec(memory_space=pl.ANY), pl.BlockSpec(memory_space=pl.ANY),
                  pl.BlockSpec((t, t), lambda bi, hp, qi: (0, 0))],
        out_specs=pl.BlockSpec((1, t, MLA_HPS * MLA_V), lambda bi, hp, qi: (bi, qi, hp)),
        out_shape=jax.ShapeDtypeStruct((b, s, MLA_HEADS * MLA_V), BF16),
        scratch_shapes=_chain_scratch(MLA_HPS, MLA_V) + [
            pltpu.VMEM((2, MLA_HPS, LANES, TQ), BF16), pltpu.VMEM((MLA_HPS, s, LANES), BF16),
            pltpu.VMEM((s // TK, MLA_HPS * MLA_V, TK), BF16), pltpu.SemaphoreType.DMA((2,))],
        compiler_params=_params("parallel", "parallel", "arbitrary"),
        name="mla_attn",
    )(qm, qm, km, vmt, mask)


def _diff_kernel(tab_ref, q_ref, qn_ref, k_hbm, vt_hbm, bias_hbm, lam_ref, g_out_ref, o_ref,
                 s_scr, cm_scr, sh_scr, m_scr, acc_scr, qt_scr, k_ref, vt_ref, bias_ref, dma_sem):
    bi = pl.program_id(0)
    hp = pl.program_id(1)
    qi = pl.program_id(2)
    far_bias = [tab_ref[NUM_BUCKETS // 2 - 1, hp * DIFF_HPS + hh] * LOG2E for hh in range(DIFF_HPS)]
    steps_per_q = TQ // KSTEP
    seq = k_ref.shape[1]
    vt_per_q = TQ // TK

    one_group = DIFF_HPS == DIFF_HEADS

    @pl.when(jnp.logical_and(qi == 0, bi == 0) if one_group else qi == 0)
    def _():
        cp = pltpu.make_async_copy(bias_hbm.at[pl.ds(hp * DIFF_HPS, DIFF_HPS)], bias_ref, dma_sem.at[2])
        cp.start()
        cp.wait()

    def chunk_copies(c):
        heads = pl.ds(hp * DIFF_HPS, DIFF_HPS)
        return [pltpu.make_async_copy(k_hbm.at[heads, pl.ds(bi * seq + c * TQ, TQ), :],
                                      k_ref.at[:, pl.ds(c * TQ, TQ), :], dma_sem.at[0]),
                pltpu.make_async_copy(vt_hbm.at[bi, pl.ds(c * vt_per_q, vt_per_q),
                                                pl.ds(hp * DIFF_HPS * DIFF_V, DIFF_HPS * DIFF_V), :],
                                      vt_ref.at[pl.ds(c * vt_per_q, vt_per_q)], dma_sem.at[1])]

    _stream_keys(qi, pl.num_programs(2), chunk_copies)

    def n_far_of(tile):
        return jnp.maximum(tile * steps_per_q - NEAR_KEYS // KSTEP, 0)

    feat = lax.broadcasted_iota(jnp.int32, (LANES, 1), 0)
    for which, ref in enumerate((q_ref, qn_ref)):
        for hh in range(DIFF_HPS):
            q_t = ref[hh].T
            zero = jnp.zeros_like(q_t)
            qt_scr[which, 2 * hh] = jnp.where(feat < DIFF_QK, q_t, zero)
            qt_scr[which, 2 * hh + 1] = jnp.where(feat >= DIFF_QK, q_t, zero)

    def chain(hh, mp):
        def scores(which, j):
            return [jnp.dot(k_ref[hh, _key_rows(j, c), :], qt_scr[which, 2 * hh + mp],
                            preferred_element_type=F32) for c in range(KSPLIT)]

        def vt(j):
            return [vt_ref[j * KSPLIT + c, hh * DIFF_V:(hh + 1) * DIFF_V, :] for c in range(KSPLIT)]

        def bias_rows(tile, j, c):
            off = pl.multiple_of((j * KSPLIT + c) * TK - (tile * TQ - NEAR_KEYS), TK)
            return bias_ref[hh, pl.ds(off, TK), :]

        return dict(qk=lambda j: scores(0, j), vt=vt, add=lambda j, c: bias_rows(qi, j, c),
                    shift=far_bias[hh], qk_next=lambda: scores(1, 0),
                    add_next=lambda c: bias_rows(qi + 1, 0, c))

    _run_chains([chain(hh, mp) for hh in range(DIFF_HPS) for mp in range(2)],
                n_far_of(qi), (qi + 1) * steps_per_q, qi == 0, qi < pl.num_programs(2) - 1,
                n_far_of(qi + 1) == 0, s_scr, cm_scr, sh_scr, m_scr, acc_scr)

    lv = lam_ref[...]
    lam = (jnp.exp(jnp.sum(lv[0:1] * lv[1:2], axis=-1, keepdims=True))
           - jnp.exp(jnp.sum(lv[2:3] * lv[3:4], axis=-1, keepdims=True)) + LAMBDA_INIT)
    for hh in range(DIFF_HPS):
        a0, l0 = acc_scr[2 * hh, :DIFF_V], acc_scr[2 * hh, DIFF_V:DIFF_V + 1]
        a1, l1 = acc_scr[2 * hh + 1, :DIFF_V], acc_scr[2 * hh + 1, DIFF_V:DIFF_V + 1]
        o_t = a0 / l0 - lam * (a1 / l1)
        ms = jnp.sum(o_t * o_t, axis=0, keepdims=True) * (1.0 / DIFF_V)
        o_t = o_t * lax.rsqrt(ms + EPS)
        o_ref[0, :, hh * LANES:(hh + 1) * LANES] = (
            o_t.T * g_out_ref[...] * (1.0 - LAMBDA_INIT)).astype(o_ref.dtype)


def _diff_attention(rel_bias, qd, kd, vdt, bias, lam_vecs, g_out, b, s):
    t = TQ
    nq = s // t
    return pl.pallas_call(
        _diff_kernel,
        grid=(b, DIFF_HEADS // DIFF_HPS, nq),
        in_specs=[pl.BlockSpec(memory_space=pltpu.SMEM),
                  pl.BlockSpec((DIFF_HPS, t, LANES), lambda bi, hd, qi: (hd, bi * nq + qi, 0)),
                  pl.BlockSpec((DIFF_HPS, t, LANES),
                               lambda bi, hd, qi: (hd, bi * nq + jnp.minimum(qi + 1, nq - 1), 0)),
                  pl.BlockSpec(memory_space=pl.ANY), pl.BlockSpec(memory_space=pl.ANY),
                  pl.BlockSpec(memory_space=pl.ANY),
                  pl.BlockSpec(lam_vecs.shape, lambda bi, hd, qi: (0, 0)),
                  pl.BlockSpec(g_out.shape, lambda bi, hd, qi: (0, 0))],
        out_specs=pl.BlockSpec((1, t, DIFF_HPS * LANES), lambda bi, hd, qi: (bi, qi, hd)),
        out_shape=jax.ShapeDtypeStruct((b, s, DIFF_HEADS * DIFF_V), BF16),
        scratch_shapes=_chain_scratch(2 * DIFF_HPS, DIFF_V) + [
            pltpu.VMEM((2, 2 * DIFF_HPS, LANES, TQ), BF16), pltpu.VMEM((DIFF_HPS, s, LANES), BF16),
            pltpu.VMEM((s // TK, DIFF_HPS * DIFF_V, TK), BF16),
            pltpu.VMEM((DIFF_HPS, NEAR_KEYS + t, t), F32), pltpu.SemaphoreType.DMA((3,))],
        compiler_params=_params("parallel", "parallel", "arbitrary"),
        name="diff_attn",
    )(rel_bias, qd, qd, kd, vdt, bias, lam_vecs, g_out)


def _ffn_kernel(tiles_per_seq, x_ref, xh_ref, ym_ref, ymh_ref, yd_ref, ydh_ref, p_ref, wom_ref, wod_ref,
                g_ffn_ref, wg_ref, wu_ref, cw_ref, cb_ref, wd_ref, g_ple_ref, wpg_ref, wpp_ref, o_ref, g_scr):
    tm = x_ref.shape[0]
    ext = lambda halo_ref, ref: jnp.concatenate([halo_ref[...], ref[...]], axis=0)
    x1_ext = (ext(xh_ref, x_ref)
              + jnp.dot(ext(ymh_ref, ym_ref), wom_ref[...], preferred_element_type=F32)
              + jnp.dot(ext(ydh_ref, yd_ref), wod_ref[...], preferred_element_type=F32))
    h2_ext = (x1_ext * _rms(x1_ext, x1_ext.shape[-1]) * g_ffn_ref[...]).astype(BF16)
    g_scr[...] = jnp.dot(h2_ext, wg_ref[...], preferred_element_type=F32)

    @pl.when(pl.program_id(0) % tiles_per_seq == 0)
    def _():
        g_scr[0:HALO_ROWS, :] = jnp.zeros((HALO_ROWS, g_scr.shape[1]), F32)

    x1 = x1_ext[HALO_ROWS:]
    h2 = h2_ext[HALO_ROWS:]
    conv = cb_ref[...]
    for j in range(CONV_WIDTH):
        start = HALO_ROWS - (CONV_WIDTH - 1) + j
        conv = conv + g_scr[start:start + tm, :] * cw_ref[j:j + 1, :]
    up = jnp.dot(h2, wu_ref[...], preferred_element_type=F32)
    act = (conv * jax.nn.sigmoid(conv) * up).astype(BF16)
    x2 = x1 + jnp.dot(act, wd_ref[...], preferred_element_type=F32)
    hn = (x2 * _rms(x2, x2.shape[-1]) * g_ple_ref[...]).astype(BF16)
    gate = jax.nn.sigmoid(jnp.dot(hn, wpg_ref[...], preferred_element_type=F32))
    proj = jnp.dot(p_ref[...].astype(BF16), wpp_ref[...], preferred_element_type=F32)
    o_ref[...] = x2 + gate * proj


def _ffn_ple(x2, ym, yd, p2, w_out_m, w_out_d, g_ffn, w_gate, w_up, conv_w, conv_b, w_down, g_ple, w_pg, w_pp,
             seq):
    n, dm = x2.shape
    tm = min(FFN_ROW_TILE, seq)
    d_ff = w_gate.shape[1]
    row = lambda i: (i, 0)
    const = lambda i: (0, 0)
    halo = lambda i: (jnp.maximum(i * (tm // HALO_ROWS) - 1, 0), 0)

    def full(a):
        return pl.BlockSpec(a.shape, const)

    def tile_and_halo(a):
        return [pl.BlockSpec((tm, a.shape[1]), row), pl.BlockSpec((HALO_ROWS, a.shape[1]), halo)]

    return pl.pallas_call(
        functools.partial(_ffn_kernel, seq // tm),
        grid=(n // tm,),
        in_specs=(tile_and_halo(x2) + tile_and_halo(ym) + tile_and_halo(yd)
                  + [pl.BlockSpec((tm, p2.shape[1]), row), full(w_out_m), full(w_out_d), full(g_ffn),
                     full(w_gate), full(w_up), full(conv_w), full(conv_b), full(w_down),
                     full(g_ple), full(w_pg), full(w_pp)]),
        out_specs=pl.BlockSpec((tm, dm), row),
        out_shape=jax.ShapeDtypeStruct((n, dm), F32),
        scratch_shapes=[pltpu.VMEM((tm + HALO_ROWS, d_ff), F32)],
        compiler_params=_params("parallel"),
        name="ffn_ple",
    )(x2, x2, ym, ym, yd, yd, p2, w_out_m, w_out_d, g_ffn, w_gate, w_up, conv_w, conv_b, w_down,
      g_ple, w_pg, w_pp)


def _head_blocks(w, width, n_heads):
    k = w.shape[0]
    w3 = w.reshape(k, n_heads, width)
    return jnp.pad(w3, ((0, 0), (0, 0), (0, LANES - width))).reshape(k, n_heads * LANES)


def _swap_rope_halves(a):
    half = MLA_ROPE // 2
    return jnp.concatenate([a[..., :MLA_NOPE], a[..., MLA_NOPE + half:MLA_QK],
                            a[..., MLA_NOPE:MLA_NOPE + half]], axis=-1)


def _lane_row(g, width=LANES):
    return jnp.pad(g, (0, width - g.shape[0])).reshape(1, width).astype(F32)


def _rope_tables(seq):
    half = MLA_ROPE // 2
    inv_freq = ROPE_THETA ** (-jnp.arange(half, dtype=F32) / half)
    ang = jnp.arange(seq, dtype=jnp.int32).astype(F32)[:, None] * inv_freq[None, :]
    cos, sin = jnp.cos(ang), jnp.sin(ang)
    ones = jnp.ones((seq, MLA_NOPE), F32)
    zeros_n = jnp.zeros((seq, MLA_NOPE), F32)
    zeros_p = jnp.zeros((seq, LANES - MLA_QK), F32)
    cos_t = jnp.concatenate([ones, cos, cos, zeros_p], axis=1)
    sin_t = jnp.concatenate([zeros_n, -sin, sin, zeros_p], axis=1)
    return cos_t, sin_t


def kernel(x, p, attn_norm_g, w_in, q_lat_norm_g, w_uq, kv_lat_norm_g, w_ukv, mla_q_norm_g, mla_k_norm_g,
           diff_q_norm_g, diff_k_norm_g, lambda_q1, lambda_k1, lambda_q2, lambda_k2, diff_out_norm_g,
           rel_bias, w_out, ffn_norm_g, w_gate, w_up, conv_w, conv_b, w_down, ple_norm_g, w_ple_gate,
           w_ple_proj):
    b, s, dm = x.shape
    depth = p.shape[0]
    assert s % TQ == 0 and s % IN_ROW_TILE == 0 and s % FFN_ROW_TILE == 0
    assert TQ % KSTEP == 0 and NEAR_KEYS % KSTEP == 0 and IN_ROW_TILE % TK == 0
    assert depth == 1

    cos_t, sin_t = _rope_tables(s)
    bias, mask = _bias_tiles(rel_bias.astype(F32))
    x2 = x.reshape(b * s, dm)

    for i in range(depth):
        wi = w_in[i]
        off_kr = MLA_Q_RANK + MLA_KV_RANK
        off_dq = off_kr + MLA_ROPE
        off_dv = off_dq + 2 * DIFF_HEADS * DIFF_V
        k_rope = wi[:, off_kr:off_dq]
        half = MLA_ROPE // 2
        k_rope_sw = jnp.concatenate([k_rope[:, half:], k_rope[:, :half]], axis=1)
        lane_pad = ((0, 0), (MLA_NOPE, LANES - MLA_QK))
        w_in_p = jnp.concatenate([wi[:, :off_kr], jnp.pad(k_rope, lane_pad), jnp.pad(k_rope_sw, lane_pad),
                                  wi[:, off_dq:off_dv]], axis=1).astype(BF16)
        w_dvt = wi[:, off_dv:].T.astype(BF16)
        w_uq_p = _head_blocks(w_uq[i], MLA_QK, MLA_HEADS).astype(BF16)
        w_uq_sw = _head_blocks(
            _swap_rope_halves(w_uq[i].reshape(MLA_Q_RANK, MLA_HEADS, MLA_QK)).reshape(MLA_Q_RANK, -1),
            MLA_QK, MLA_HEADS).astype(BF16)
        w_ukv3 = w_ukv[i].reshape(MLA_KV_RANK, MLA_HEADS, MLA_NOPE + MLA_V)
        w_uk_p = _head_blocks(w_ukv3[:, :, :MLA_NOPE].reshape(MLA_KV_RANK, -1), MLA_NOPE, MLA_HEADS).astype(BF16)
        w_uvt = w_ukv3[:, :, MLA_NOPE:].reshape(MLA_KV_RANK, MLA_HEADS * MLA_V).T.astype(BF16)
        gq, gk = mla_q_norm_g[i], mla_k_norm_g[i]
        gdq = jnp.tile(diff_q_norm_g[i], 2).reshape(1, LANES).astype(F32)
        gdk = jnp.tile(diff_k_norm_g[i], 2).reshape(1, LANES).astype(F32)

        qm, km, vmt, qd, kd, vdt = _in_proj(
            x2, cos_t, sin_t, attn_norm_g[i].reshape(1, dm), w_in_p, w_dvt, q_lat_norm_g[i].reshape(1, -1),
            w_uq_p, w_uq_sw, kv_lat_norm_g[i].reshape(1, -1), w_uk_p, w_uvt,
            _lane_row(gq), _lane_row(_swap_rope_halves(gq)), _lane_row(gk), _lane_row(_swap_rope_halves(gk)),
            gdq, gdk, b, s)

        y_mla = _mla_attention(qm, km, vmt, mask, b, s)
        lam_vecs = jnp.stack([lambda_q1[i], lambda_k1[i], lambda_q2[i], lambda_k2[i]]).astype(F32)
        y_diff = _diff_attention(rel_bias.astype(F32), qd, kd, vdt, bias, lam_vecs,
                                 diff_out_norm_g[i].reshape(1, DIFF_V).astype(F32), b, s)

        n_mla = MLA_HEADS * MLA_V
        x2 = _ffn_ple(x2, y_mla.reshape(b * s, -1), y_diff.reshape(b * s, -1), p[i].reshape(b * s, -1),
                      w_out[i][:n_mla].astype(BF16), w_out[i][n_mla:].astype(BF16),
                      ffn_norm_g[i].reshape(1, dm), w_gate[i].astype(BF16), w_up[i].astype(BF16),
                      conv_w[i], conv_b[i].reshape(1, -1), w_down[i].astype(BF16),
                      ple_norm_g[i].reshape(1, dm), w_ple_gate[i].astype(BF16), w_ple_proj[i].astype(BF16), s)
    return x2.reshape(b, s, dm)
```

```python
import functools
import math

import jax
import jax.numpy as jnp
from jax import lax
from jax.experimental import pallas as pl
from jax.experimental.pallas import tpu as pltpu

F32 = jnp.float32
BF16 = jnp.bfloat16

LANES = 128
SUBLANES = 8
VMEM_LIMIT_BYTES = 56 * 1024 * 1024

CHUNK = 64
EPS = 1e-6
NEG_INF = -1e30
MLA_HEADS = 8
MLA_Q_RANK = 256
MLA_KV_RANK = 128
MLA_NOPE = 64
MLA_ROPE = 32
MLA_QK = MLA_NOPE + MLA_ROPE
MLA_V = 64
ROPE_THETA = 10000.0
DIFF_HEADS = 4
DIFF_QK = 64
DIFF_V = 2 * DIFF_QK
NUM_BUCKETS = 32
MAX_DISTANCE = 1024
CONV_WIDTH = 3
LAMBDA_INIT = 0.8 - 0.6 * math.exp(-0.3 * 0)
LOG2E = math.log2(math.e)

TQ = 512
TK = 512
KSPLIT = 1
KSTEP = TK * KSPLIT
SUM_ROWS = 16
NEAR_KEYS = 1024
MLA_HPS = 8
DIFF_HPS = 4
IN_ROW_TILE = 512
FFN_ROW_TILE = 512
HALO_ROWS = 2 * SUBLANES

ZC_QLAT = 0
ZC_KVLAT = ZC_QLAT + MLA_Q_RANK
ZC_KR = ZC_KVLAT + MLA_KV_RANK
ZC_KRSW = ZC_KR + LANES
ZC_DQ = ZC_KRSW + LANES
ZC_DK = ZC_DQ + DIFF_HEADS * DIFF_V
ZC_END = ZC_DK + DIFF_HEADS * DIFF_V

NT_DIMS = (((1,), (1,)), ((), ()))


def _params(*sem):
    return pltpu.CompilerParams(dimension_semantics=sem, vmem_limit_bytes=VMEM_LIMIT_BYTES)


def _rms(x, width):
    return lax.rsqrt(jnp.sum(x * x, axis=-1, keepdims=True) * (1.0 / width) + EPS)


def _bias_kernel(tab_ref, bias_ref, mask_ref):
    t = TQ
    kk = lax.broadcasted_iota(jnp.int32, (t, t), 0)
    qq = lax.broadcasted_iota(jnp.int32, (t, t), 1)
    chunk_bits = CHUNK.bit_length() - 1
    q_chunk = lax.shift_right_logical(qq, chunk_bits)
    mask_ref[...] = jnp.where(lax.shift_right_logical(kk, chunk_bits) <= q_chunk, 0.0, NEG_INF).astype(F32)
    key_off = kk + pl.program_id(0) * t - NEAR_KEYS
    add_mask = jnp.where(lax.shift_right_arithmetic(key_off, chunk_bits) <= q_chunk, 0.0, NEG_INF).astype(F32)
    rel = key_off - qq
    nb = NUM_BUCKETS // 2
    max_exact = nb // 2
    sign_off = (rel > 0).astype(jnp.int32) * nb
    n = jnp.abs(rel)
    nf = jnp.maximum(n, 1).astype(F32)
    large = max_exact + jnp.floor(jnp.log(nf / max_exact) / math.log(MAX_DISTANCE / max_exact)
                                  * (nb - max_exact)).astype(jnp.int32)
    large = jnp.minimum(large, nb - 1)
    bucket = sign_off + jnp.where(n < max_exact, n, large)
    for h in range(DIFF_HEADS):
        acc = jnp.zeros((t, t), F32)
        for b in range(NUM_BUCKETS):
            acc = jnp.where(bucket == b, tab_ref[b, h], acc)
        bias_ref[h] = acc * LOG2E + add_mask


def _bias_tiles(rel_bias):
    t = TQ
    n_blocks = (NEAR_KEYS + TQ) // t
    return pl.pallas_call(
        _bias_kernel,
        grid=(n_blocks,),
        in_specs=[pl.BlockSpec(memory_space=pltpu.SMEM)],
        out_specs=[pl.BlockSpec((DIFF_HEADS, t, t), lambda d: (0, d, 0)),
                   pl.BlockSpec((t, t), lambda d: (0, 0))],
        out_shape=[jax.ShapeDtypeStruct((DIFF_HEADS, n_blocks * t, t), F32),
                   jax.ShapeDtypeStruct((t, t), F32)],
        compiler_params=_params("arbitrary"),
        name="bias_tiles",
    )(rel_bias)


def _in_proj_kernel(x_ref, cos_ref, sin_ref, g_attn_ref, w_in_ref, w_dvt_ref, g_ql_ref, w_uq_ref, w_uqsw_ref,
                    g_kvl_ref, w_uk_ref, w_uvt_ref, gq_ref, gqsw_ref, gk_ref, gksw_ref,
                    gdq_ref, gdk_ref,
                    qm_ref, km_ref, vmt_ref, qd_ref, kd_ref, vdt_ref):
    x = x_ref[...]
    h = (x * _rms(x, x.shape[-1]) * g_attn_ref[...]).astype(BF16)
    z = jnp.dot(h, w_in_ref[:, :ZC_DQ], preferred_element_type=F32)

    def store_transposed(dst_ref, w_t, act):
        v_t = lax.dot_general(w_t, act, NT_DIMS, preferred_element_type=F32).astype(BF16)
        for c in range(v_t.shape[1] // TK):
            dst_ref[0, c] = v_t[:, c * TK:(c + 1) * TK]

    q_lat = z[:, ZC_QLAT:ZC_KVLAT]
    kv_lat = z[:, ZC_KVLAT:ZC_KR]
    kr = z[:, ZC_KR:ZC_KRSW]
    krsw = z[:, ZC_KRSW:ZC_DQ]

    qln = (q_lat * _rms(q_lat, MLA_Q_RANK) * g_ql_ref[...]).astype(BF16)
    q = jnp.dot(qln, w_uq_ref[...], preferred_element_type=F32)
    qsw = jnp.dot(qln, w_uqsw_ref[...], preferred_element_type=F32)
    kvn = (kv_lat * _rms(kv_lat, MLA_KV_RANK) * g_kvl_ref[...]).astype(BF16)
    kn = jnp.dot(kvn, w_uk_ref[...], preferred_element_type=F32)
    store_transposed(vmt_ref, w_uvt_ref[...], kvn)
    z_d = jnp.dot(h, w_in_ref[:, ZC_DQ:], preferred_element_type=F32)
    store_transposed(vdt_ref, w_dvt_ref[...], h)

    cos = cos_ref[...]
    sin = sin_ref[...]
    q_scale = MLA_QK ** -0.5 * LOG2E
    q_cos = cos * (gq_ref[...] * q_scale)
    q_sin = sin * (gqsw_ref[...] * q_scale)
    k_cos = cos * gk_ref[...]
    k_sin = sin * gksw_ref[...]
    for hd in range(MLA_HEADS):
        sl = slice(hd * LANES, (hd + 1) * LANES)
        qh = q[:, sl]
        qm_ref[hd] = (_rms(qh, MLA_QK) * (qh * q_cos + qsw[:, sl] * q_sin)).astype(BF16)
        kh = kn[:, sl] + kr
        km_ref[hd] = (_rms(kh, MLA_QK) * (kh * k_cos + krsw * k_sin)).astype(BF16)

    lane = lax.broadcasted_iota(jnp.int32, (1, LANES), 1)
    first_map = lane < DIFF_QK
    d_scale = DIFF_QK ** -0.5 * LOG2E
    for hd in range(DIFF_HEADS):
        sl = slice(hd * LANES, (hd + 1) * LANES)
        for src, g_ref, dst, scale in ((ZC_DQ, gdq_ref, qd_ref, d_scale), (ZC_DK, gdk_ref, kd_ref, 1.0)):
            blk = z_d[:, src - ZC_DQ + hd * LANES: src - ZC_DQ + (hd + 1) * LANES]
            sq = blk * blk
            tot = jnp.sum(sq, axis=-1, keepdims=True)
            lo = jnp.sum(jnp.where(first_map, sq, 0.0), axis=-1, keepdims=True)
            ms = jnp.where(first_map, lo, tot - lo) * (1.0 / DIFF_QK)
            dst[hd] = (blk * lax.rsqrt(ms + EPS) * (g_ref[...] * scale)).astype(BF16)


def _in_proj(x2, cos_t, sin_t, g_attn, w_in_p, w_dvt, g_ql, w_uq_p, w_uq_sw, g_kvl, w_uk_p, w_uvt,
             gq, gqsw, gk, gksw, gdq, gdk, batch, seq):
    n, dm = x2.shape
    tm = IN_ROW_TILE
    tiles_per_seq = seq // tm
    row = lambda i: (i, 0)
    const = lambda i: (0, 0)
    pos = lambda i: (i % tiles_per_seq, 0)
    tile4 = lambda i: (i // tiles_per_seq, i % tiles_per_seq, 0, 0)
    kt = tm // TK

    def full(a):
        return pl.BlockSpec(a.shape, const)

    def heads_out(heads):
        return (pl.BlockSpec((heads, tm, LANES), lambda i: (0, i, 0)),
                jax.ShapeDtypeStruct((heads, n, LANES), BF16))

    def transposed_out(width):
        return (pl.BlockSpec((1, kt, width, TK), tile4),
                jax.ShapeDtypeStruct((batch, tiles_per_seq * kt, width, TK), BF16))

    outs = [heads_out(MLA_HEADS), heads_out(MLA_HEADS), transposed_out(MLA_HEADS * MLA_V),
            heads_out(DIFF_HEADS), heads_out(DIFF_HEADS), transposed_out(DIFF_HEADS * DIFF_V)]
    return pl.pallas_call(
        _in_proj_kernel,
        grid=(n // tm,),
        in_specs=[pl.BlockSpec((tm, dm), row), pl.BlockSpec((tm, LANES), pos), pl.BlockSpec((tm, LANES), pos),
                  full(g_attn), full(w_in_p), full(w_dvt), full(g_ql), full(w_uq_p), full(w_uq_sw),
                  full(g_kvl), full(w_uk_p), full(w_uvt), full(gq), full(gqsw), full(gk), full(gksw),
                  full(gdq), full(gdk)],
        out_specs=[o[0] for o in outs],
        out_shape=[o[1] for o in outs],
        compiler_params=_params("parallel"),
        name="in_proj",
    )(x2, cos_t, sin_t, g_attn, w_in_p, w_dvt, g_ql, w_uq_p, w_uq_sw, g_kvl, w_uk_p, w_uvt,
      gq, gqsw, gk, gksw, gdq, gdk)


def _col_max(s_tiles):
    return functools.reduce(jnp.maximum, [jnp.max(s, axis=0, keepdims=True) for s in s_tiles])


def _key_rows(step_idx, c):
    return pl.ds(pl.multiple_of(step_idx * KSTEP + c * TK, TK), TK)


def _run_chains(chains, n_plain, n_all, first, has_next, next_decorated, s_scr, cm_scr, sh_scr, m_scr, acc_scr):
    row = lax.broadcasted_iota(jnp.int32, (SUM_ROWS, TK), 0)
    ones_rows = jnp.where(row == 0, 1.0, 0.0).astype(BF16)
    use_shift = any(ch["shift"] is not None for ch in chains)

    def park(ci, s_tiles, add_tiles):
        shift = chains[ci]["shift"]
        if add_tiles is not None:
            s_tiles = [s + a for s, a in zip(s_tiles, add_tiles)]
        for c, s in enumerate(s_tiles):
            s_scr[ci, c] = s
        owed = shift if (shift is not None and add_tiles is None) else 0.0
        cm_scr[ci] = functools.reduce(jnp.maximum, [jnp.max(s, axis=0, keepdims=True) for s in s_tiles]) + owed
        if use_shift:
            sh_scr[ci] = jnp.zeros(sh_scr.shape[1:], F32) + owed

    def produce(ci, j, decorated):
        ch = chains[ci]
        park(ci, ch["qk"](j), [ch["add"](j, c) for c in range(KSPLIT)] if decorated else None)

    def produce_next(ci, decorated):
        ch = chains[ci]
        park(ci, ch["qk_next"](), [ch["add_next"](c) for c in range(KSPLIT)] if decorated else None)

    def consume(ci, j):
        vt = chains[ci]["vt"]
        m = m_scr[ci]
        m_new = jnp.maximum(m, cm_scr[ci])
        alpha = jnp.exp2(m - m_new)
        m_sub = m_new - sh_scr[ci] if use_shift else m_new
        acc = alpha * acc_scr[ci]
        for c, v_t in enumerate(vt(j)):
            p = jnp.exp2(s_scr[ci, c] - m_sub).astype(BF16)
            acc = acc + jnp.dot(jnp.concatenate([v_t, ones_rows], axis=0), p, preferred_element_type=F32)
        m_scr[ci] = m_new
        acc_scr[ci] = acc

    n_chains = len(chains)
    for ci in range(n_chains):
        m_scr[ci] = jnp.full(m_scr.shape[1:], NEG_INF, F32)
        acc_scr[ci] = jnp.zeros(acc_scr.shape[1:], F32)

    @pl.when(first)
    def _():
        for ci in range(n_chains):
            produce(ci, 0, True)

    def run(start, stop, decorated):
        def body(j, carry):
            for ci in range(n_chains):
                consume(ci, j)
                produce(ci, j + 1, decorated)
            return carry
        lax.fori_loop(start, stop, body, 0)

    switch = jnp.maximum(n_plain - 1, 0)
    run(0, switch, False)
    run(switch, n_all - 1, True)

    last_variants = [(jnp.logical_not(has_next), None)]
    if next_decorated is False:
        last_variants.append((has_next, False))
    else:
        last_variants += [(jnp.logical_and(has_next, jnp.logical_not(next_decorated)), False),
                          (jnp.logical_and(has_next, next_decorated), True)]
    for pred, kind in last_variants:
        @pl.when(pred)
        def _(kind=kind):
            for ci in range(n_chains):
                consume(ci, n_all - 1)
                if kind is not None:
                    produce_next(ci, kind)


def _chain_scratch(n_chains, dv):
    stat = pltpu.VMEM((n_chains, 1, TQ), F32)
    return [pltpu.VMEM((n_chains, KSPLIT, TK, TQ), F32), stat, stat, stat,
            pltpu.VMEM((n_chains, dv + SUM_ROWS, TQ), F32)]


def _stream_keys(qi, n_tiles, chunk_copies):
    @pl.when(qi == 0)
    def _():
        for cp in chunk_copies(0):
            cp.start()

    for cp in chunk_copies(qi):
        cp.wait()

    @pl.when(qi + 1 < n_tiles)
    def _():
        for cp in chunk_copies(qi + 1):
            cp.start()


def _mla_kernel(q_ref, qn_ref, k_hbm, vt_hbm, mask_ref, o_ref, s_scr, cm_scr, sh_scr, m_scr, acc_scr, qt_scr,
                k_ref, vt_ref, dma_sem):
    bi = pl.program_id(0)
    hp = pl.program_id(1)
    qi = pl.program_id(2)
    steps_per_q = TQ // KSTEP
    n_all = (qi + 1) * steps_per_q
    seq = k_ref.shape[1]
    vt_per_q = TQ // TK

    def chunk_copies(c):
        heads = pl.ds(hp * MLA_HPS, MLA_HPS)
        return [pltpu.make_async_copy(k_hbm.at[heads, pl.ds(bi * seq + c * TQ, TQ), :],
                                      k_ref.at[:, pl.ds(c * TQ, TQ), :], dma_sem.at[0]),
                pltpu.make_async_copy(vt_hbm.at[bi, pl.ds(c * vt_per_q, vt_per_q),
                                                pl.ds(hp * MLA_HPS * MLA_V, MLA_HPS * MLA_V), :],
                                      vt_ref.at[pl.ds(c * vt_per_q, vt_per_q)], dma_sem.at[1])]

    _stream_keys(qi, pl.num_programs(2), chunk_copies)

    def mask(j, c):
        return mask_ref[pl.ds(pl.multiple_of((j * KSPLIT + c) * TK - qi * TQ, TK), TK), :]

    slots = (lax.rem(qi, 2), 1 - lax.rem(qi, 2))

    @pl.when(qi == 0)
    def _():
        for hh in range(MLA_HPS):
            qt_scr[0, hh] = q_ref[hh].T

    for hh in range(MLA_HPS):
        qt_scr[slots[1], hh] = qn_ref[hh].T

    def chain(hh):
        def scores(which, j):
            return [jnp.dot(k_ref[hh, _key_rows(j, c), :], qt_scr[slots[which], hh],
                            preferred_element_type=F32) for c in range(KSPLIT)]

        def vt(j):
            return [vt_ref[j * KSPLIT + c, hh * MLA_V:(hh + 1) * MLA_V, :] for c in range(KSPLIT)]

        return dict(qk=lambda j: scores(0, j), vt=vt, add=mask, shift=None,
                    qk_next=lambda: scores(1, 0), add_next=None)

    _run_chains([chain(hh) for hh in range(MLA_HPS)], qi * steps_per_q, n_all,
                qi == 0, qi < pl.num_programs(2) - 1, False, s_scr, cm_scr, sh_scr, m_scr, acc_scr)
    o_t = jnp.concatenate([acc_scr[hh, :MLA_V] / acc_scr[hh, MLA_V:MLA_V + 1] for hh in range(MLA_HPS)], axis=0)
    o_ref[0] = o_t.T.astype(o_ref.dtype)


def _mla_attention(qm, km, vmt, mask, b, s):
    t = TQ
    nq = s // t
    pairs = MLA_HEADS // MLA_HPS
    return pl.pallas_call(
        _mla_kernel,
        grid=(b, pairs, nq),
        in_specs=[pl.BlockSpec((MLA_HPS, t, LANES), lambda bi, hp, qi: (hp, bi * nq + qi, 0)),
                  pl.BlockSpec((MLA_HPS, t, LANES),
                               lambda bi, hp, qi: (hp, bi * nq + jnp.minimum(qi + 1, nq - 1), 0)),
                  pl.BlockSpec(memory_space=pl.ANY), pl.BlockSpec(memory_space=pl.ANY),
                  pl.BlockSpec((t, t), lambda bi, hp, qi: (0, 0))],
        out_specs=pl.BlockSpec((1, t, MLA_HPS * MLA_V), lambda bi, hp, qi: (bi, qi, hp)),
        out_shape=jax.ShapeDtypeStruct((b, s, MLA_HEADS * MLA_V), BF16),
        scratch_shapes=_chain_scratch(MLA_HPS, MLA_V) + [
            pltpu.VMEM((2, MLA_HPS, LANES, TQ), BF16), pltpu.VMEM((MLA_HPS, s, LANES), BF16),
            pltpu.VMEM((s // TK, MLA_HPS * MLA_V, TK), BF16), pltpu.SemaphoreType.DMA((2,))],
        compiler_params=_params("parallel", "parallel", "arbitrary"),
        name="mla_attn",
    )(qm, qm, km, vmt, mask)


def _diff_kernel(tab_ref, q_ref, qn_ref, k_hbm, vt_hbm, bias_hbm, lam_ref, g_out_ref, o_ref,
                 s_scr, cm_scr, sh_scr, m_scr, acc_scr, qt_scr, k_ref, vt_ref, bias_ref, dma_sem):
    bi = pl.program_id(0)
    hp = pl.program_id(1)
    qi = pl.program_id(2)
    far_bias = [tab_ref[NUM_BUCKETS // 2 - 1, hp * DIFF_HPS + hh] * LOG2E for hh in range(DIFF_HPS)]
    steps_per_q = TQ // KSTEP
    seq = k_ref.shape[1]
    vt_per_q = TQ // TK

    one_group = DIFF_HPS == DIFF_HEADS

    @pl.when(jnp.logical_and(qi == 0, bi == 0) if one_group else qi == 0)
    def _():
        cp = pltpu.make_async_copy(bias_hbm.at[pl.ds(hp * DIFF_HPS, DIFF_HPS)], bias_ref, dma_sem.at[2])
        cp.start()
        cp.wait()

    def chunk_copies(c):
        heads = pl.ds(hp * DIFF_HPS, DIFF_HPS)
        return [pltpu.make_async_copy(k_hbm.at[heads, pl.ds(bi * seq + c * TQ, TQ), :],
                                      k_ref.at[:, pl.ds(c * TQ, TQ), :], dma_sem.at[0]),
                pltpu.make_async_copy(vt_hbm.at[bi, pl.ds(c * vt_per_q, vt_per_q),
                                                pl.ds(hp * DIFF_HPS * DIFF_V, DIFF_HPS * DIFF_V), :],
                                      vt_ref.at[pl.ds(c * vt_per_q, vt_per_q)], dma_sem.at[1])]

    _stream_keys(qi, pl.num_programs(2), chunk_copies)

    def n_far_of(tile):
        return jnp.maximum(tile * steps_per_q - NEAR_KEYS // KSTEP, 0)

    feat = lax.broadcasted_iota(jnp.int32, (LANES, 1), 0)
    slots = (lax.rem(qi, 2), 1 - lax.rem(qi, 2))

    def transpose_into(slot, ref):
        for hh in range(DIFF_HPS):
            q_t = ref[hh].T
            zero = jnp.zeros_like(q_t)
            qt_scr[slot, 2 * hh] = jnp.where(feat < DIFF_QK, q_t, zero)
            qt_scr[slot, 2 * hh + 1] = jnp.where(feat >= DIFF_QK, q_t, zero)

    @pl.when(qi == 0)
    def _():
        transpose_into(0, q_ref)

    transpose_into(slots[1], qn_ref)

    def chain(hh, mp):
        def scores(which, j):
            return [jnp.dot(k_ref[hh, _key_rows(j, c), :], qt_scr[slots[which], 2 * hh + mp],
                            preferred_element_type=F32) for c in range(KSPLIT)]

        def vt(j):
            return [vt_ref[j * KSPLIT + c, hh * DIFF_V:(hh + 1) * DIFF_V, :] for c in range(KSPLIT)]

        def bias_rows(tile, j, c):
            off = pl.multiple_of((j * KSPLIT + c) * TK - (tile * TQ - NEAR_KEYS), TK)
            return bias_ref[hh, pl.ds(off, TK), :]

        return dict(qk=lambda j: scores(0, j), vt=vt, add=lambda j, c: bias_rows(qi, j, c),
                    shift=far_bias[hh], qk_next=lambda: scores(1, 0),
                    add_next=lambda c: bias_rows(qi + 1, 0, c))

    _run_chains([chain(hh, mp) for hh in range(DIFF_HPS) for mp in range(2)],
                n_far_of(qi), (qi + 1) * steps_per_q, qi == 0, qi < pl.num_programs(2) - 1,
                n_far_of(qi + 1) == 0, s_scr, cm_scr, sh_scr, m_scr, acc_scr)

    lv = lam_ref[...]
    lam = (jnp.exp(jnp.sum(lv[0:1] * lv[1:2], axis=-1, keepdims=True))
           - jnp.exp(jnp.sum(lv[2:3] * lv[3:4], axis=-1, keepdims=True)) + LAMBDA_INIT)
    for hh in range(DIFF_HPS):
        a0, l0 = acc_scr[2 * hh, :DIFF_V], acc_scr[2 * hh, DIFF_V:DIFF_V + 1]
        a1, l1 = acc_scr[2 * hh + 1, :DIFF_V], acc_scr[2 * hh + 1, DIFF_V:DIFF_V + 1]
        o_t = a0 / l0 - lam * (a1 / l1)
        ms = jnp.sum(o_t * o_t, axis=0, keepdims=True) * (1.0 / DIFF_V)
        o_t = o_t * lax.rsqrt(ms + EPS)
        o_ref[0, :, hh * LANES:(hh + 1) * LANES] = (
            o_t.T * g_out_ref[...] * (1.0 - LAMBDA_INIT)).astype(o_ref.dtype)


def _diff_attention(rel_bias, qd, kd, vdt, bias, lam_vecs, g_out, b, s):
    t = TQ
    nq = s // t
    return pl.pallas_call(
        _diff_kernel,
        grid=(b, DIFF_HEADS // DIFF_HPS, nq),
        in_specs=[pl.BlockSpec(memory_space=pltpu.SMEM),
                  pl.BlockSpec((DIFF_HPS, t, LANES), lambda bi, hd, qi: (hd, bi * nq + qi, 0)),
                  pl.BlockSpec((DIFF_HPS, t, LANES),
                               lambda bi, hd, qi: (hd, bi * nq + jnp.minimum(qi + 1, nq - 1), 0)),
                  pl.BlockSpec(memory_space=pl.ANY), pl.BlockSpec(memory_space=pl.ANY),
                  pl.BlockSpec(memory_space=pl.ANY),
                  pl.BlockSpec(lam_vecs.shape, lambda bi, hd, qi: (0, 0)),
                  pl.BlockSpec(g_out.shape, lambda bi, hd, qi: (0, 0))],
        out_specs=pl.BlockSpec((1, t, DIFF_HPS * LANES), lambda bi, hd, qi: (bi, qi, hd)),
        out_shape=jax.ShapeDtypeStruct((b, s, DIFF_HEADS * DIFF_V), BF16),
        scratch_shapes=_chain_scratch(2 * DIFF_HPS, DIFF_V) + [
            pltpu.VMEM((2, 2 * DIFF_HPS, LANES, TQ), BF16), pltpu.VMEM((DIFF_HPS, s, LANES), BF16),
            pltpu.VMEM((s // TK, DIFF_HPS * DIFF_V, TK), BF16),
            pltpu.VMEM((DIFF_HPS, NEAR_KEYS + t, t), F32), pltpu.SemaphoreType.DMA((3,))],
        compiler_params=_params("parallel", "parallel", "arbitrary"),
        name="diff_attn",
    )(rel_bias, qd, qd, kd, vdt, bias, lam_vecs, g_out)


def _ffn_kernel(tiles_per_seq, x_ref, xh_ref, ym_ref, ymh_ref, yd_ref, ydh_ref, p_ref, wom_ref, wod_ref,
                g_ffn_ref, wg_ref, wu_ref, cw_ref, cb_ref, wd_ref, g_ple_ref, wpg_ref, wpp_ref, o_ref, g_scr):
    tm = x_ref.shape[0]
    ext = lambda halo_ref, ref: jnp.concatenate([halo_ref[...], ref[...]], axis=0)
    x1_ext = (ext(xh_ref, x_ref)
              + jnp.dot(ext(ymh_ref, ym_ref), wom_ref[...], preferred_element_type=F32)
              + jnp.dot(ext(ydh_ref, yd_ref), wod_ref[...], preferred_element_type=F32))
    h2_ext = (x1_ext * _rms(x1_ext, x1_ext.shape[-1]) * g_ffn_ref[...]).astype(BF16)
    g_scr[...] = jnp.dot(h2_ext, wg_ref[...], preferred_element_type=F32)

    @pl.when(pl.program_id(0) % tiles_per_seq == 0)
    def _():
        g_scr[0:HALO_ROWS, :] = jnp.zeros((HALO_ROWS, g_scr.shape[1]), F32)

    x1 = x1_ext[HALO_ROWS:]
    h2 = h2_ext[HALO_ROWS:]
    conv = cb_ref[...]
    for j in range(CONV_WIDTH):
        start = HALO_ROWS - (CONV_WIDTH - 1) + j
        conv = conv + g_scr[start:start + tm, :] * cw_ref[j:j + 1, :]
    up = jnp.dot(h2, wu_ref[...], preferred_element_type=F32)
    act = (conv * jax.nn.sigmoid(conv) * up).astype(BF16)
    x2 = x1 + jnp.dot(act, wd_ref[...], preferred_element_type=F32)
    hn = (x2 * _rms(x2, x2.shape[-1]) * g_ple_ref[...]).astype(BF16)
    gate = jax.nn.sigmoid(jnp.dot(hn, wpg_ref[...], preferred_element_type=F32))
    proj = jnp.dot(p_ref[...].astype(BF16), wpp_ref[...], preferred_element_type=F32)
    o_ref[...] = x2 + gate * proj


def _ffn_ple(x2, ym, yd, p2, w_out_m, w_out_d, g_ffn, w_gate, w_up, conv_w, conv_b, w_down, g_ple, w_pg, w_pp,
             seq):
    n, dm = x2.shape
    tm = min(FFN_ROW_TILE, seq)
    d_ff = w_gate.shape[1]
    row = lambda i: (i, 0)
    const = lambda i: (0, 0)
    halo = lambda i: (jnp.maximum(i * (tm // HALO_ROWS) - 1, 0), 0)

    def full(a):
        return pl.BlockSpec(a.shape, const)

    def tile_and_halo(a):
        return [pl.BlockSpec((tm, a.shape[1]), row), pl.BlockSpec((HALO_ROWS, a.shape[1]), halo)]

    return pl.pallas_call(
        functools.partial(_ffn_kernel, seq // tm),
        grid=(n // tm,),
        in_specs=(tile_and_halo(x2) + tile_and_halo(ym) + tile_and_halo(yd)
                  + [pl.BlockSpec((tm, p2.shape[1]), row), full(w_out_m), full(w_out_d), full(g_ffn),
                     full(w_gate), full(w_up), full(conv_w), full(conv_b), full(w_down),
                     full(g_ple), full(w_pg), full(w_pp)]),
        out_specs=pl.BlockSpec((tm, dm), row),
        out_shape=jax.ShapeDtypeStruct((n, dm), F32),
        scratch_shapes=[pltpu.VMEM((tm + HALO_ROWS, d_ff), F32)],
        compiler_params=_params("parallel"),
        name="ffn_ple",
    )(x2, x2, ym, ym, yd, yd, p2, w_out_m, w_out_d, g_ffn, w_gate, w_up, conv_w, conv_b, w_down,
      g_ple, w_pg, w_pp)


def _head_blocks(w, width, n_heads):
    k = w.shape[0]
    w3 = w.reshape(k, n_heads, width)
    return jnp.pad(w3, ((0, 0), (0, 0), (0, LANES - width))).reshape(k, n_heads * LANES)


def _swap_rope_halves(a):
    half = MLA_ROPE // 2
    return jnp.concatenate([a[..., :MLA_NOPE], a[..., MLA_NOPE + half:MLA_QK],
                            a[..., MLA_NOPE:MLA_NOPE + half]], axis=-1)


def _lane_row(g, width=LANES):
    return jnp.pad(g, (0, width - g.shape[0])).reshape(1, width).astype(F32)


def _rope_tables(seq):
    half = MLA_ROPE // 2
    inv_freq = ROPE_THETA ** (-jnp.arange(half, dtype=F32) / half)
    ang = jnp.arange(seq, dtype=jnp.int32).astype(F32)[:, None] * inv_freq[None, :]
    cos, sin = jnp.cos(ang), jnp.sin(ang)
    ones = jnp.ones((seq, MLA_NOPE), F32)
    zeros_n = jnp.zeros((seq, MLA_NOPE), F32)
    zeros_p = jnp.zeros((seq, LANES - MLA_QK), F32)
    cos_t = jnp.concatenate([ones, cos, cos, zeros_p], axis=1)
    sin_t = jnp.concatenate([zeros_n, -sin, sin, zeros_p], axis=1)
    return cos_t, sin_t


def kernel(x, p, attn_norm_g, w_in, q_lat_norm_g, w_uq, kv_lat_norm_g, w_ukv, mla_q_norm_g, mla_k_norm_g,
           diff_q_norm_g, diff_k_norm_g, lambda_q1, lambda_k1, lambda_q2, lambda_k2, diff_out_norm_g,
           rel_bias, w_out, ffn_norm_g, w_gate, w_up, conv_w, conv_b, w_down, ple_norm_g, w_ple_gate,
           w_ple_proj):
    b, s, dm = x.shape
    depth = p.shape[0]
    assert s % TQ == 0 and s % IN_ROW_TILE == 0 and s % FFN_ROW_TILE == 0
    assert TQ % KSTEP == 0 and NEAR_KEYS % KSTEP == 0 and IN_ROW_TILE % TK == 0
    assert depth == 1

    cos_t, sin_t = _rope_tables(s)
    bias, mask = _bias_tiles(rel_bias.astype(F32))
    x2 = x.reshape(b * s, dm)

    for i in range(depth):
        wi = w_in[i]
        off_kr = MLA_Q_RANK + MLA_KV_RANK
        off_dq = off_kr + MLA_ROPE
        off_dv = off_dq + 2 * DIFF_HEADS * DIFF_V
        k_rope = wi[:, off_kr:off_dq]
        half = MLA_ROPE // 2
        k_rope_sw = jnp.concatenate([k_rope[:, half:], k_rope[:, :half]], axis=1)
        lane_pad = ((0, 0), (MLA_NOPE, LANES - MLA_QK))
        w_in_p = jnp.concatenate([wi[:, :off_kr], jnp.pad(k_rope, lane_pad), jnp.pad(k_rope_sw, lane_pad),
                                  wi[:, off_dq:off_dv]], axis=1).astype(BF16)
        w_dvt = wi[:, off_dv:].T.astype(BF16)
        w_uq_p = _head_blocks(w_uq[i], MLA_QK, MLA_HEADS).astype(BF16)
        w_uq_sw = _head_blocks(
            _swap_rope_halves(w_uq[i].reshape(MLA_Q_RANK, MLA_HEADS, MLA_QK)).reshape(MLA_Q_RANK, -1),
            MLA_QK, MLA_HEADS).astype(BF16)
        w_ukv3 = w_ukv[i].reshape(MLA_KV_RANK, MLA_HEADS, MLA_NOPE + MLA_V)
        w_uk_p = _head_blocks(w_ukv3[:, :, :MLA_NOPE].reshape(MLA_KV_RANK, -1), MLA_NOPE, MLA_HEADS).astype(BF16)
        w_uvt = w_ukv3[:, :, MLA_NOPE:].reshape(MLA_KV_RANK, MLA_HEADS * MLA_V).T.astype(BF16)
        gq, gk = mla_q_norm_g[i], mla_k_norm_g[i]
        gdq = jnp.tile(diff_q_norm_g[i], 2).reshape(1, LANES).astype(F32)
        gdk = jnp.tile(diff_k_norm_g[i], 2).reshape(1, LANES).astype(F32)

        qm, km, vmt, qd, kd, vdt = _in_proj(
            x2, cos_t, sin_t, attn_norm_g[i].reshape(1, dm), w_in_p, w_dvt, q_lat_norm_g[i].reshape(1, -1),
            w_uq_p, w_uq_sw, kv_lat_norm_g[i].reshape(1, -1), w_uk_p, w_uvt,
            _lane_row(gq), _lane_row(_swap_rope_halves(gq)), _lane_row(gk), _lane_row(_swap_rope_halves(gk)),
            gdq, gdk, b, s)

        y_mla = _mla_attention(qm, km, vmt, mask, b, s)
        lam_vecs = jnp.stack([lambda_q1[i], lambda_k1[i], lambda_q2[i], lambda_k2[i]]).astype(F32)
        y_diff = _diff_attention(rel_bias.astype(F32), qd, kd, vdt, bias, lam_vecs,
                                 diff_out_norm_g[i].reshape(1, DIFF_V).astype(F32), b, s)

        n_mla = MLA_HEADS * MLA_V
        x2 = _ffn_ple(x2, y_mla.reshape(b * s, -1), y_diff.reshape(b * s, -1), p[i].reshape(b * s, -1),
                      w_out[i][:n_mla].astype(BF16), w_out[i][n_mla:].astype(BF16),
                      ffn_norm_g[i].reshape(1, dm), w_gate[i].astype(BF16), w_up[i].astype(BF16),
                      conv_w[i], conv_b[i].reshape(1, -1), w_down[i].astype(BF16),
                      ple_norm_g[i].reshape(1, dm), w_ple_gate[i].astype(BF16), w_ple_proj[i].astype(BF16), s)
    return x2.reshape(b, s, dm)
```

```python
import functools
import math

import jax
import jax.numpy as jnp
from jax import lax
from jax.experimental import pallas as pl
from jax.experimental.pallas import tpu as pltpu

F32 = jnp.float32
BF16 = jnp.bfloat16

LANES = 128
SUBLANES = 8
VMEM_LIMIT_BYTES = 56 * 1024 * 1024

CHUNK = 64
EPS = 1e-6
NEG_INF = -1e30
MLA_HEADS = 8
MLA_Q_RANK = 256
MLA_KV_RANK = 128
MLA_NOPE = 64
MLA_ROPE = 32
MLA_QK = MLA_NOPE + MLA_ROPE
MLA_V = 64
ROPE_THETA = 10000.0
DIFF_HEADS = 4
DIFF_QK = 64
DIFF_V = 2 * DIFF_QK
NUM_BUCKETS = 32
MAX_DISTANCE = 1024
CONV_WIDTH = 3
LAMBDA_INIT = 0.8 - 0.6 * math.exp(-0.3 * 0)
LOG2E = math.log2(math.e)

TQ = 512
TK = 512
KSPLIT = 1
KSTEP = TK * KSPLIT
SUM_ROWS = 16
NEAR_KEYS = 1024
MLA_HPS = 8
DIFF_HPS = 4
IN_ROW_TILE = 512
FFN_ROW_TILE = 512
HALO_ROWS = 2 * SUBLANES

ZC_QLAT = 0
ZC_KVLAT = ZC_QLAT + MLA_Q_RANK
ZC_KR = ZC_KVLAT + MLA_KV_RANK
ZC_KRSW = ZC_KR + LANES
ZC_DQ = ZC_KRSW + LANES
ZC_DK = ZC_DQ + DIFF_HEADS * DIFF_V
ZC_END = ZC_DK + DIFF_HEADS * DIFF_V

NT_DIMS = (((1,), (1,)), ((), ()))


def _params(*sem):
    return pltpu.CompilerParams(dimension_semantics=sem, vmem_limit_bytes=VMEM_LIMIT_BYTES)


def _rms(x, width):
    return lax.rsqrt(jnp.sum(x * x, axis=-1, keepdims=True) * (1.0 / width) + EPS)


def _bias_kernel(tab_ref, bias_ref, mask_ref):
    t = TQ
    kk = lax.broadcasted_iota(jnp.int32, (t, t), 0)
    qq = lax.broadcasted_iota(jnp.int32, (t, t), 1)
    chunk_bits = CHUNK.bit_length() - 1
    q_chunk = lax.shift_right_logical(qq, chunk_bits)
    mask_ref[...] = jnp.where(lax.shift_right_logical(kk, chunk_bits) <= q_chunk, 0.0, NEG_INF).astype(F32)
    key_off = kk + pl.program_id(0) * t - NEAR_KEYS
    add_mask = jnp.where(lax.shift_right_arithmetic(key_off, chunk_bits) <= q_chunk, 0.0, NEG_INF).astype(F32)
    rel = key_off - qq
    nb = NUM_BUCKETS // 2
    max_exact = nb // 2
    sign_off = (rel > 0).astype(jnp.int32) * nb
    n = jnp.abs(rel)
    nf = jnp.maximum(n, 1).astype(F32)
    large = max_exact + jnp.floor(jnp.log(nf / max_exact) / math.log(MAX_DISTANCE / max_exact)
                                  * (nb - max_exact)).astype(jnp.int32)
    large = jnp.minimum(large, nb - 1)
    bucket = sign_off + jnp.where(n < max_exact, n, large)
    for h in range(DIFF_HEADS):
        acc = jnp.zeros((t, t), F32)
        for b in range(NUM_BUCKETS):
            acc = jnp.where(bucket == b, tab_ref[b, h], acc)
        bias_ref[h] = acc * LOG2E + add_mask


def _bias_tiles(rel_bias):
    t = TQ
    n_blocks = (NEAR_KEYS + TQ) // t
    return pl.pallas_call(
        _bias_kernel,
        grid=(n_blocks,),
        in_specs=[pl.BlockSpec(memory_space=pltpu.SMEM)],
        out_specs=[pl.BlockSpec((DIFF_HEADS, t, t), lambda d: (0, d, 0)),
                   pl.BlockSpec((t, t), lambda d: (0, 0))],
        out_shape=[jax.ShapeDtypeStruct((DIFF_HEADS, n_blocks * t, t), F32),
                   jax.ShapeDtypeStruct((t, t), F32)],
        compiler_params=_params("arbitrary"),
        name="bias_tiles",
    )(rel_bias)


def _in_proj_kernel(x_ref, cos_ref, sin_ref, g_attn_ref, w_in_ref, w_dvt_ref, g_ql_ref, w_uq_ref, w_uqsw_ref,
                    g_kvl_ref, w_uk_ref, w_uvt_ref, gq_ref, gqsw_ref, gk_ref, gksw_ref,
                    gdq_ref, gdk_ref,
                    qm_ref, km_ref, vmt_ref, qd_ref, kd_ref, vdt_ref):
    x = x_ref[...]
    h = (x * _rms(x, x.shape[-1]) * g_attn_ref[...]).astype(BF16)
    z = jnp.dot(h, w_in_ref[:, :ZC_DQ], preferred_element_type=F32)

    def store_transposed(dst_ref, w_t, act):
        v_t = lax.dot_general(w_t, act, NT_DIMS, preferred_element_type=F32).astype(BF16)
        for c in range(v_t.shape[1] // TK):
            dst_ref[0, c] = v_t[:, c * TK:(c + 1) * TK]

    q_lat = z[:, ZC_QLAT:ZC_KVLAT]
    kv_lat = z[:, ZC_KVLAT:ZC_KR]
    kr = z[:, ZC_KR:ZC_KRSW]
    krsw = z[:, ZC_KRSW:ZC_DQ]

    qln = (q_lat * _rms(q_lat, MLA_Q_RANK) * g_ql_ref[...]).astype(BF16)
    q = jnp.dot(qln, w_uq_ref[...], preferred_element_type=F32)
    qsw = jnp.dot(qln, w_uqsw_ref[...], preferred_element_type=F32)
    kvn = (kv_lat * _rms(kv_lat, MLA_KV_RANK) * g_kvl_ref[...]).astype(BF16)
    kn = jnp.dot(kvn, w_uk_ref[...], preferred_element_type=F32)
    store_transposed(vmt_ref, w_uvt_ref[...], kvn)
    z_d = jnp.dot(h, w_in_ref[:, ZC_DQ:], preferred_element_type=F32)
    store_transposed(vdt_ref, w_dvt_ref[...], h)

    cos = cos_ref[...]
    sin = sin_ref[...]
    q_scale = MLA_QK ** -0.5 * LOG2E
    q_cos = cos * (gq_ref[...] * q_scale)
    q_sin = sin * (gqsw_ref[...] * q_scale)
    k_cos = cos * gk_ref[...]
    k_sin = sin * gksw_ref[...]
    for hd in range(MLA_HEADS):
        sl = slice(hd * LANES, (hd + 1) * LANES)
        qh = q[:, sl]
        qm_ref[hd] = (_rms(qh, MLA_QK) * (qh * q_cos + qsw[:, sl] * q_sin)).astype(BF16)
        kh = kn[:, sl] + kr
        km_ref[hd] = (_rms(kh, MLA_QK) * (kh * k_cos + krsw * k_sin)).astype(BF16)

    lane = lax.broadcasted_iota(jnp.int32, (1, LANES), 1)
    first_map = lane < DIFF_QK
    d_scale = DIFF_QK ** -0.5 * LOG2E
    for hd in range(DIFF_HEADS):
        sl = slice(hd * LANES, (hd + 1) * LANES)
        for src, g_ref, dst, scale in ((ZC_DQ, gdq_ref, qd_ref, d_scale), (ZC_DK, gdk_ref, kd_ref, 1.0)):
            blk = z_d[:, src - ZC_DQ + hd * LANES: src - ZC_DQ + (hd + 1) * LANES]
            sq = blk * blk
            tot = jnp.sum(sq, axis=-1, keepdims=True)
            lo = jnp.sum(jnp.where(first_map, sq, 0.0), axis=-1, keepdims=True)
            ms = jnp.where(first_map, lo, tot - lo) * (1.0 / DIFF_QK)
            dst[hd] = (blk * lax.rsqrt(ms + EPS) * (g_ref[...] * scale)).astype(BF16)


def _in_proj(x2, cos_t, sin_t, g_attn, w_in_p, w_dvt, g_ql, w_uq_p, w_uq_sw, g_kvl, w_uk_p, w_uvt,
             gq, gqsw, gk, gksw, gdq, gdk, batch, seq):
    n, dm = x2.shape
    tm = IN_ROW_TILE
    tiles_per_seq = seq // tm
    row = lambda i: (i, 0)
    const = lambda i: (0, 0)
    pos = lambda i: (i % tiles_per_seq, 0)
    tile4 = lambda i: (i // tiles_per_seq, i % tiles_per_seq, 0, 0)
    kt = tm // TK

    def full(a):
        return pl.BlockSpec(a.shape, const)

    def heads_out(heads):
        return (pl.BlockSpec((heads, tm, LANES), lambda i: (0, i, 0)),
                jax.ShapeDtypeStruct((heads, n, LANES), BF16))

    def transposed_out(width):
        return (pl.BlockSpec((1, kt, width, TK), tile4),
                jax.ShapeDtypeStruct((batch, tiles_per_seq * kt, width, TK), BF16))

    outs = [heads_out(MLA_HEADS), heads_out(MLA_HEADS), transposed_out(MLA_HEADS * MLA_V),
            heads_out(DIFF_HEADS), heads_out(DIFF_HEADS), transposed_out(DIFF_HEADS * DIFF_V)]
    return pl.pallas_call(
        _in_proj_kernel,
        grid=(n // tm,),
        in_specs=[pl.BlockSpec((tm, dm), row), pl.BlockSpec((tm, LANES), pos), pl.BlockSpec((tm, LANES), pos),
                  full(g_attn), full(w_in_p), full(w_dvt), full(g_ql), full(w_uq_p), full(w_uq_sw),
                  full(g_kvl), full(w_uk_p), full(w_uvt), full(gq), full(gqsw), full(gk), full(gksw),
                  full(gdq), full(gdk)],
        out_specs=[o[0] for o in outs],
        out_shape=[o[1] for o in outs],
        compiler_params=_params("parallel"),
        name="in_proj",
    )(x2, cos_t, sin_t, g_attn, w_in_p, w_dvt, g_ql, w_uq_p, w_uq_sw, g_kvl, w_uk_p, w_uvt,
      gq, gqsw, gk, gksw, gdq, gdk)


def _key_rows(step_idx, c):
    return pl.ds(pl.multiple_of(step_idx * KSTEP + c * TK, TK), TK)


def _run_chains(chains, n_plain, n_all, first, has_next, next_decorated, s_scr, cm_scr, sh_scr, m_scr, acc_scr):
    row = lax.broadcasted_iota(jnp.int32, (SUM_ROWS, TK), 0)
    ones_rows = jnp.where(row == 0, 1.0, 0.0).astype(BF16)
    use_shift = any(ch["shift"] is not None for ch in chains)

    def park(ci, s_tiles, add_tiles):
        shift = chains[ci]["shift"]
        if add_tiles is not None:
            s_tiles = [s + a for s, a in zip(s_tiles, add_tiles)]
        for c, s in enumerate(s_tiles):
            s_scr[ci, c] = s
        owed = shift if (shift is not None and add_tiles is None) else 0.0
        cm_scr[ci] = functools.reduce(jnp.maximum, [jnp.max(s, axis=0, keepdims=True) for s in s_tiles]) + owed
        if use_shift:
            sh_scr[ci] = jnp.zeros(sh_scr.shape[1:], F32) + owed

    def produce(ci, j, decorated):
        ch = chains[ci]
        park(ci, ch["qk"](j), [ch["add"](j, c) for c in range(KSPLIT)] if decorated else None)

    def produce_next(ci, decorated):
        ch = chains[ci]
        park(ci, ch["qk_next"](), [ch["add_next"](c) for c in range(KSPLIT)] if decorated else None)

    def consume(ci, j):
        vt = chains[ci]["vt"]
        m = m_scr[ci]
        m_new = jnp.maximum(m, cm_scr[ci])
        alpha = jnp.exp2(m - m_new)
        m_sub = m_new - sh_scr[ci] if use_shift else m_new
        acc = alpha * acc_scr[ci]
        for c, v_t in enumerate(vt(j)):
            p = jnp.exp2(s_scr[ci, c] - m_sub).astype(BF16)
            acc = acc + jnp.dot(jnp.concatenate([v_t, ones_rows], axis=0), p, preferred_element_type=F32)
        m_scr[ci] = m_new
        acc_scr[ci] = acc

    n_chains = len(chains)
    for ci in range(n_chains):
        m_scr[ci] = jnp.full(m_scr.shape[1:], NEG_INF, F32)
        acc_scr[ci] = jnp.zeros(acc_scr.shape[1:], F32)

    @pl.when(first)
    def _():
        for ci in range(n_chains):
            produce(ci, 0, True)

    def run(start, stop, decorated):
        def body(j, carry):
            for ci in range(n_chains):
                consume(ci, j)
                produce(ci, j + 1, decorated)
            return carry
        lax.fori_loop(start, stop, body, 0)

    switch = jnp.maximum(n_plain - 1, 0)
    run(0, switch, False)
    run(switch, n_all - 1, True)

    last_variants = [(jnp.logical_not(has_next), None)]
    if next_decorated is False:
        last_variants.append((has_next, False))
    else:
        last_variants += [(jnp.logical_and(has_next, jnp.logical_not(next_decorated)), False),
                          (jnp.logical_and(has_next, next_decorated), True)]
    for pred, kind in last_variants:
        @pl.when(pred)
        def _(kind=kind):
            for ci in range(n_chains):
                consume(ci, n_all - 1)
                if kind is not None:
                    produce_next(ci, kind)


def _chain_scratch(n_chains, dv):
    stat = pltpu.VMEM((n_chains, 1, TQ), F32)
    return [pltpu.VMEM((n_chains, KSPLIT, TK, TQ), F32), stat, stat, stat,
            pltpu.VMEM((n_chains, dv + SUM_ROWS, TQ), F32)]


def _stream_keys(qi, n_tiles, chunk_copies):
    @pl.when(qi == 0)
    def _():
        for cp in chunk_copies(0):
            cp.start()

    for cp in chunk_copies(qi):
        cp.wait()

    @pl.when(qi + 1 < n_tiles)
    def _():
        for cp in chunk_copies(qi + 1):
            cp.start()


def _mla_kernel(q_ref, qn_ref, k_hbm, vt_hbm, mask_ref, o_ref, s_scr, cm_scr, sh_scr, m_scr, acc_scr, qt_scr,
                k_ref, vt_ref, dma_sem):
    bi = pl.program_id(0)
    hp = pl.program_id(1)
    qi = pl.program_id(2)
    steps_per_q = TQ // KSTEP
    n_all = (qi + 1) * steps_per_q
    seq = k_ref.shape[1]
    vt_per_q = TQ // TK

    def chunk_copies(c):
        heads = pl.ds(hp * MLA_HPS, MLA_HPS)
        return [pltpu.make_async_copy(k_hbm.at[heads, pl.ds(bi * seq + c * TQ, TQ), :],
                                      k_ref.at[:, pl.ds(c * TQ, TQ), :], dma_sem.at[0]),
                pltpu.make_async_copy(vt_hbm.at[bi, pl.ds(c * vt_per_q, vt_per_q),
                                                pl.ds(hp * MLA_HPS * MLA_V, MLA_HPS * MLA_V), :],
                                      vt_ref.at[pl.ds(c * vt_per_q, vt_per_q)], dma_sem.at[1])]

    _stream_keys(qi, pl.num_programs(2), chunk_copies)

    def mask(j, c):
        return mask_ref[pl.ds(pl.multiple_of((j * KSPLIT + c) * TK - qi * TQ, TK), TK), :]

    slots = (lax.rem(qi, 2), 1 - lax.rem(qi, 2))

    @pl.when(qi == 0)
    def _():
        for hh in range(MLA_HPS):
            qt_scr[0, hh] = q_ref[hh].T

    for hh in range(MLA_HPS):
        qt_scr[slots[1], hh] = qn_ref[hh].T

    def chain(hh):
        def scores(which, j):
            return [jnp.dot(k_ref[hh, _key_rows(j, c), :], qt_scr[slots[which], hh],
                            preferred_element_type=F32) for c in range(KSPLIT)]

        def vt(j):
            return [vt_ref[j * KSPLIT + c, hh * MLA_V:(hh + 1) * MLA_V, :] for c in range(KSPLIT)]

        return dict(qk=lambda j: scores(0, j), vt=vt, add=mask, shift=None,
                    qk_next=lambda: scores(1, 0), add_next=None)

    _run_chains([chain(hh) for hh in range(MLA_HPS)], qi * steps_per_q, n_all,
                qi == 0, qi < pl.num_programs(2) - 1, False, s_scr, cm_scr, sh_scr, m_scr, acc_scr)
    o_t = jnp.concatenate([acc_scr[hh, :MLA_V] / acc_scr[hh, MLA_V:MLA_V + 1] for hh in range(MLA_HPS)], axis=0)
    o_ref[0] = o_t.T.astype(o_ref.dtype)


def _mla_attention(qm, km, vmt, mask, b, s):
    t = TQ
    nq = s // t
    pairs = MLA_HEADS // MLA_HPS
    return pl.pallas_call(
        _mla_kernel,
        grid=(b, pairs, nq),
        in_specs=[pl.BlockSpec((MLA_HPS, t, LANES), lambda bi, hp, qi: (hp, bi * nq + qi, 0)),
                  pl.BlockSpec((MLA_HPS, t, LANES),
                               lambda bi, hp, qi: (hp, bi * nq + jnp.minimum(qi + 1, nq - 1), 0)),
                  pl.BlockSpec(memory_space=pl.ANY), pl.BlockSpec(memory_space=pl.ANY),
                  pl.BlockSpec((t, t), lambda bi, hp, qi: (0, 0))],
        out_specs=pl.BlockSpec((1, t, MLA_HPS * MLA_V), lambda bi, hp, qi: (bi, qi, hp)),
        out_shape=jax.ShapeDtypeStruct((b, s, MLA_HEADS * MLA_V), BF16),
        scratch_shapes=_chain_scratch(MLA_HPS, MLA_V) + [
            pltpu.VMEM((2, MLA_HPS, LANES, TQ), BF16), pltpu.VMEM((MLA_HPS, s, LANES), BF16),
            pltpu.VMEM((s // TK, MLA_HPS * MLA_V, TK), BF16), pltpu.SemaphoreType.DMA((2,))],
        compiler_params=_params("parallel", "parallel", "arbitrary"),
        name="mla_attn",
    )(qm, qm, km, vmt, mask)


def _diff_kernel(tab_ref, q_ref, qn_ref, k_hbm, vt_hbm, bias_hbm, lam_ref, g_out_ref, o_ref,
                 s_scr, cm_scr, sh_scr, m_scr, acc_scr, qt_scr, k_ref, vt_ref, bias_ref, dma_sem):
    bi = pl.program_id(0)
    hp = pl.program_id(1)
    qi = pl.program_id(2)
    far_bias = [tab_ref[NUM_BUCKETS // 2 - 1, hp * DIFF_HPS + hh] * LOG2E for hh in range(DIFF_HPS)]
    steps_per_q = TQ // KSTEP
    seq = k_ref.shape[1]
    vt_per_q = TQ // TK

    one_group = DIFF_HPS == DIFF_HEADS

    @pl.when(jnp.logical_and(qi == 0, bi == 0) if one_group else qi == 0)
    def _():
        cp = pltpu.make_async_copy(bias_hbm.at[pl.ds(hp * DIFF_HPS, DIFF_HPS)], bias_ref, dma_sem.at[2])
        cp.start()
        cp.wait()

    def chunk_copies(c):
        heads = pl.ds(hp * DIFF_HPS, DIFF_HPS)
        return [pltpu.make_async_copy(k_hbm.at[heads, pl.ds(bi * seq + c * TQ, TQ), :],
                                      k_ref.at[:, pl.ds(c * TQ, TQ), :], dma_sem.at[0]),
                pltpu.make_async_copy(vt_hbm.at[bi, pl.ds(c * vt_per_q, vt_per_q),
                                                pl.ds(hp * DIFF_HPS * DIFF_V, DIFF_HPS * DIFF_V), :],
                                      vt_ref.at[pl.ds(c * vt_per_q, vt_per_q)], dma_sem.at[1])]

    _stream_keys(qi, pl.num_programs(2), chunk_copies)

    def n_far_of(tile):
        return jnp.maximum(tile * steps_per_q - NEAR_KEYS // KSTEP, 0)

    feat = lax.broadcasted_iota(jnp.int32, (LANES, 1), 0)
    slots = (lax.rem(qi, 2), 1 - lax.rem(qi, 2))

    def transpose_into(slot, ref):
        for hh in range(DIFF_HPS):
            q_t = ref[hh].T
            zero = jnp.zeros_like(q_t)
            qt_scr[slot, 2 * hh] = jnp.where(feat < DIFF_QK, q_t, zero)
            qt_scr[slot, 2 * hh + 1] = jnp.where(feat >= DIFF_QK, q_t, zero)

    @pl.when(qi == 0)
    def _():
        transpose_into(0, q_ref)

    transpose_into(slots[1], qn_ref)

    def chain(hh, mp):
        def scores(which, j):
            return [jnp.dot(k_ref[hh, _key_rows(j, c), :], qt_scr[slots[which], 2 * hh + mp],
                            preferred_element_type=F32) for c in range(KSPLIT)]

        def vt(j):
            return [vt_ref[j * KSPLIT + c, hh * DIFF_V:(hh + 1) * DIFF_V, :] for c in range(KSPLIT)]

        def bias_rows(tile, j, c):
            off = pl.multiple_of((j * KSPLIT + c) * TK - (tile * TQ - NEAR_KEYS), TK)
            return bias_ref[hh, pl.ds(off, TK), :]

        return dict(qk=lambda j: scores(0, j), vt=vt, add=lambda j, c: bias_rows(qi, j, c),
                    shift=far_bias[hh], qk_next=lambda: scores(1, 0),
                    add_next=lambda c: bias_rows(qi + 1, 0, c))

    _run_chains([chain(hh, mp) for hh in range(DIFF_HPS) for mp in range(2)],
                n_far_of(qi), (qi + 1) * steps_per_q, qi == 0, qi < pl.num_programs(2) - 1,
                n_far_of(qi + 1) == 0, s_scr, cm_scr, sh_scr, m_scr, acc_scr)

    lv = lam_ref[...]
    lam = (jnp.exp(jnp.sum(lv[0:1] * lv[1:2], axis=-1, keepdims=True))
           - jnp.exp(jnp.sum(lv[2:3] * lv[3:4], axis=-1, keepdims=True)) + LAMBDA_INIT)
    for hh in range(DIFF_HPS):
        a0, l0 = acc_scr[2 * hh, :DIFF_V], acc_scr[2 * hh, DIFF_V:DIFF_V + 1]
        a1, l1 = acc_scr[2 * hh + 1, :DIFF_V], acc_scr[2 * hh + 1, DIFF_V:DIFF_V + 1]
        o_t = a0 / l0 - lam * (a1 / l1)
        ms = jnp.sum(o_t * o_t, axis=0, keepdims=True) * (1.0 / DIFF_V)
        o_t = o_t * lax.rsqrt(ms + EPS)
        o_ref[0, :, hh * LANES:(hh + 1) * LANES] = (
            o_t.T * g_out_ref[...] * (1.0 - LAMBDA_INIT)).astype(o_ref.dtype)


def _diff_attention(rel_bias, qd, kd, vdt, bias, lam_vecs, g_out, b, s):
    t = TQ
    nq = s // t
    return pl.pallas_call(
        _diff_kernel,
        grid=(b, DIFF_HEADS // DIFF_HPS, nq),
        in_specs=[pl.BlockSpec(memory_space=pltpu.SMEM),
                  pl.BlockSpec((DIFF_HPS, t, LANES), lambda bi, hd, qi: (hd, bi * nq + qi, 0)),
                  pl.BlockSpec((DIFF_HPS, t, LANES),
                               lambda bi, hd, qi: (hd, bi * nq + jnp.minimum(qi + 1, nq - 1), 0)),
                  pl.BlockSpec(memory_space=pl.ANY), pl.BlockSpec(memory_space=pl.ANY),
                  pl.BlockSpec(memory_space=pl.ANY),
                  pl.BlockSpec(lam_vecs.shape, lambda bi, hd, qi: (0, 0)),
                  pl.BlockSpec(g_out.shape, lambda bi, hd, qi: (0, 0))],
        out_specs=pl.BlockSpec((1, t, DIFF_HPS * LANES), lambda bi, hd, qi: (bi, qi, hd)),
        out_shape=jax.ShapeDtypeStruct((b, s, DIFF_HEADS * DIFF_V), BF16),
        scratch_shapes=_chain_scratch(2 * DIFF_HPS, DIFF_V) + [
            pltpu.VMEM((2, 2 * DIFF_HPS, LANES, TQ), BF16), pltpu.VMEM((DIFF_HPS, s, LANES), BF16),
            pltpu.VMEM((s // TK, DIFF_HPS * DIFF_V, TK), BF16),
            pltpu.VMEM((DIFF_HPS, NEAR_KEYS + t, t), F32), pltpu.SemaphoreType.DMA((3,))],
        compiler_params=_params("parallel", "parallel", "arbitrary"),
        name="diff_attn",
    )(rel_bias, qd, qd, kd, vdt, bias, lam_vecs, g_out)


def _ffn_kernel(tiles_per_seq, x_ref, xh_ref, ym_ref, ymh_ref, yd_ref, ydh_ref, p_ref, wom_ref, wod_ref,
                g_ffn_ref, wg_ref, wu_ref, cw_ref, cb_ref, wd_ref, g_ple_ref, wpg_ref, wpp_ref, o_ref, g_scr):
    tm = x_ref.shape[0]
    ext = lambda halo_ref, ref: jnp.concatenate([halo_ref[...], ref[...]], axis=0)
    x1_ext = (ext(xh_ref, x_ref)
              + jnp.dot(ext(ymh_ref, ym_ref), wom_ref[...], preferred_element_type=F32)
              + jnp.dot(ext(ydh_ref, yd_ref), wod_ref[...], preferred_element_type=F32))
    h2_ext = (x1_ext * _rms(x1_ext, x1_ext.shape[-1]) * g_ffn_ref[...]).astype(BF16)
    g_scr[...] = jnp.dot(h2_ext, wg_ref[...], preferred_element_type=F32)

    @pl.when(pl.program_id(0) % tiles_per_seq == 0)
    def _():
        g_scr[0:HALO_ROWS, :] = jnp.zeros((HALO_ROWS, g_scr.shape[1]), F32)

    x1 = x1_ext[HALO_ROWS:]
    h2 = h2_ext[HALO_ROWS:]
    conv = cb_ref[...]
    for j in range(CONV_WIDTH):
        start = HALO_ROWS - (CONV_WIDTH - 1) + j
        conv = conv + g_scr[start:start + tm, :] * cw_ref[j:j + 1, :]
    up = jnp.dot(h2, wu_ref[...], preferred_element_type=F32)
    act = (conv * jax.nn.sigmoid(conv) * up).astype(BF16)
    x2 = x1 + jnp.dot(act, wd_ref[...], preferred_element_type=F32)
    hn = (x2 * _rms(x2, x2.shape[-1]) * g_ple_ref[...]).astype(BF16)
    gate = jax.nn.sigmoid(jnp.dot(hn, wpg_ref[...], preferred_element_type=F32))
    proj = jnp.dot(p_ref[...].astype(BF16), wpp_ref[...], preferred_element_type=F32)
    o_ref[...] = x2 + gate * proj


def _ffn_ple(x2, ym, yd, p2, w_out_m, w_out_d, g_ffn, w_gate, w_up, conv_w, conv_b, w_down, g_ple, w_pg, w_pp,
             seq):
    n, dm = x2.shape
    tm = min(FFN_ROW_TILE, seq)
    d_ff = w_gate.shape[1]
    row = lambda i: (i, 0)
    const = lambda i: (0, 0)
    halo = lambda i: (jnp.maximum(i * (tm // HALO_ROWS) - 1, 0), 0)

    def full(a):
        return pl.BlockSpec(a.shape, const)

    def tile_and_halo(a):
        return [pl.BlockSpec((tm, a.shape[1]), row), pl.BlockSpec((HALO_ROWS, a.shape[1]), halo)]

    return pl.pallas_call(
        functools.partial(_ffn_kernel, seq // tm),
        grid=(n // tm,),
        in_specs=(tile_and_halo(x2) + tile_and_halo(ym) + tile_and_halo(yd)
                  + [pl.BlockSpec((tm, p2.shape[1]), row), full(w_out_m), full(w_out_d), full(g_ffn),
                     full(w_gate), full(w_up), full(conv_w), full(conv_b), full(w_down),
                     full(g_ple), full(w_pg), full(w_pp)]),
        out_specs=pl.BlockSpec((tm, dm), row),
        out_shape=jax.ShapeDtypeStruct((n, dm), F32),
        scratch_shapes=[pltpu.VMEM((tm + HALO_ROWS, d_ff), F32)],
        compiler_params=_params("parallel"),
        name="ffn_ple",
    )(x2, x2, ym, ym, yd, yd, p2, w_out_m, w_out_d, g_ffn, w_gate, w_up, conv_w, conv_b, w_down,
      g_ple, w_pg, w_pp)


def _head_blocks(w, width, n_heads):
    k = w.shape[0]
    w3 = w.reshape(k, n_heads, width)
    return jnp.pad(w3, ((0, 0), (0, 0), (0, LANES - width))).reshape(k, n_heads * LANES)


def _swap_rope_halves(a):
    half = MLA_ROPE // 2
    return jnp.concatenate([a[..., :MLA_NOPE], a[..., MLA_NOPE + half:MLA_QK],
                            a[..., MLA_NOPE:MLA_NOPE + half]], axis=-1)


def _lane_row(g, width=LANES):
    return jnp.pad(g, (0, width - g.shape[0])).reshape(1, width).astype(F32)


def _rope_tables(seq):
    half = MLA_ROPE // 2
    inv_freq = ROPE_THETA ** (-jnp.arange(half, dtype=F32) / half)
    ang = jnp.arange(seq, dtype=jnp.int32).astype(F32)[:, None] * inv_freq[None, :]
    cos, sin = jnp.cos(ang), jnp.sin(ang)
    ones = jnp.ones((seq, MLA_NOPE), F32)
    zeros_n = jnp.zeros((seq, MLA_NOPE), F32)
    zeros_p = jnp.zeros((seq, LANES - MLA_QK), F32)
    cos_t = jnp.concatenate([ones, cos, cos, zeros_p], axis=1)
    sin_t = jnp.concatenate([zeros_n, -sin, sin, zeros_p], axis=1)
    return cos_t, sin_t


def kernel(x, p, attn_norm_g, w_in, q_lat_norm_g, w_uq, kv_lat_norm_g, w_ukv, mla_q_norm_g, mla_k_norm_g,
           diff_q_norm_g, diff_k_norm_g, lambda_q1, lambda_k1, lambda_q2, lambda_k2, diff_out_norm_g,
           rel_bias, w_out, ffn_norm_g, w_gate, w_up, conv_w, conv_b, w_down, ple_norm_g, w_ple_gate,
           w_ple_proj):
    b, s, dm = x.shape
    depth = p.shape[0]
    assert s % TQ == 0 and s % IN_ROW_TILE == 0 and s % FFN_ROW_TILE == 0
    assert TQ % KSTEP == 0 and NEAR_KEYS % KSTEP == 0 and IN_ROW_TILE % TK == 0
    assert depth == 1

    cos_t, sin_t = _rope_tables(s)
    bias, mask = _bias_tiles(rel_bias.astype(F32))
    x2 = x.reshape(b * s, dm)

    for i in range(depth):
        wi = w_in[i]
        off_kr = MLA_Q_RANK + MLA_KV_RANK
        off_dq = off_kr + MLA_ROPE
        off_dv = off_dq + 2 * DIFF_HEADS * DIFF_V
        k_rope = wi[:, off_kr:off_dq]
        half = MLA_ROPE // 2
        k_rope_sw = jnp.concatenate([k_rope[:, half:], k_rope[:, :half]], axis=1)
        lane_pad = ((0, 0), (MLA_NOPE, LANES - MLA_QK))
        w_in_p = jnp.concatenate([wi[:, :off_kr], jnp.pad(k_rope, lane_pad), jnp.pad(k_rope_sw, lane_pad),
                                  wi[:, off_dq:off_dv]], axis=1).astype(BF16)
        w_dvt = wi[:, off_dv:].T.astype(BF16)
        w_uq_p = _head_blocks(w_uq[i], MLA_QK, MLA_HEADS).astype(BF16)
        w_uq_sw = _head_blocks(
            _swap_rope_halves(w_uq[i].reshape(MLA_Q_RANK, MLA_HEADS, MLA_QK)).reshape(MLA_Q_RANK, -1),
            MLA_QK, MLA_HEADS).astype(BF16)
        w_ukv3 = w_ukv[i].reshape(MLA_KV_RANK, MLA_HEADS, MLA_NOPE + MLA_V)
        w_uk_p = _head_blocks(w_ukv3[:, :, :MLA_NOPE].reshape(MLA_KV_RANK, -1), MLA_NOPE, MLA_HEADS).astype(BF16)
        w_uvt = w_ukv3[:, :, MLA_NOPE:].reshape(MLA_KV_RANK, MLA_HEADS * MLA_V).T.astype(BF16)
        gq, gk = mla_q_norm_g[i], mla_k_norm_g[i]
        gdq = jnp.tile(diff_q_norm_g[i], 2).reshape(1, LANES).astype(F32)
        gdk = jnp.tile(diff_k_norm_g[i], 2).reshape(1, LANES).astype(F32)

        qm, km, vmt, qd, kd, vdt = _in_proj(
            x2, cos_t, sin_t, attn_norm_g[i].reshape(1, dm), w_in_p, w_dvt, q_lat_norm_g[i].reshape(1, -1),
            w_uq_p, w_uq_sw, kv_lat_norm_g[i].reshape(1, -1), w_uk_p, w_uvt,
            _lane_row(gq), _lane_row(_swap_rope_halves(gq)), _lane_row(gk), _lane_row(_swap_rope_halves(gk)),
            gdq, gdk, b, s)

        y_mla = _mla_attention(qm, km, vmt, mask, b, s)
        lam_vecs = jnp.stack([lambda_q1[i], lambda_k1[i], lambda_q2[i], lambda_k2[i]]).astype(F32)
        y_diff = _diff_attention(rel_bias.astype(F32), qd, kd, vdt, bias, lam_vecs,
                                 diff_out_norm_g[i].reshape(1, DIFF_V).astype(F32), b, s)

        n_mla = MLA_HEADS * MLA_V
        x2 = _ffn_ple(x2, y_mla.reshape(b * s, -1), y_diff.reshape(b * s, -1), p[i].reshape(b * s, -1),
                      w_out[i][:n_mla].astype(BF16), w_out[i][n_mla:].astype(BF16),
                      ffn_norm_g[i].reshape(1, dm), w_gate[i].astype(BF16), w_up[i].astype(BF16),
                      conv_w[i], conv_b[i].reshape(1, -1), w_down[i].astype(BF16),
                      ple_norm_g[i].reshape(1, dm), w_ple_gate[i].astype(BF16), w_ple_proj[i].astype(BF16), s)
    return x2.reshape(b, s, dm)
```
